```python
import math
import jax, jax.numpy as jnp
from jax import lax
import numpy as np

D_MODEL = 2048
BATCH = 4
SEQ = 4096
DEPTH = 1

CHUNK = 64
Q_BLOCK = 128
RMS_EPS = 1e-6

A_HEADS = 8
A_QK_DIM = 64
A_V_DIM = 2 * A_QK_DIM
A_WIDTH = A_HEADS * A_V_DIM
Q_COLS = A_HEADS * 2 * A_QK_DIM
K_COLS = Q_COLS
V_COLS = A_WIDTH

LRU_WIDTH = D_MODEL // 2
LRU_BLOCKS = 8
LRU_BLOCK_DIM = LRU_WIDTH // LRU_BLOCKS
CONV_WIDTH = 4
LRU_C = 8.0

REL_BUCKETS = 32
REL_MAX_DIST = 128

N_BRANCH = 2
IN_COLS = Q_COLS + K_COLS + V_COLS + 2 * LRU_WIDTH + N_BRANCH * D_MODEL

N_EXPERTS = 32
TOP_K = 4
D_EXPERT = D_MODEL
SWIGLU_LIMIT = 7.0
SWIGLU_ALPHA = 1.702
MOE_BLOCK = 128

kernel_name = 'hybrid_diffattn_rglru_moe_block'


def rmsnorm(x, g):
    xf = x.astype(jnp.float32)
    y = xf * lax.rsqrt(jnp.mean(xf * xf, axis=-1, keepdims=True) + RMS_EPS)
    return (y * g.astype(jnp.float32)).astype(x.dtype)


def t5_bucket(rel):
    nb = REL_BUCKETS // 2
    max_exact = nb // 2
    ret = jnp.where(rel > 0, nb, 0)
    n = jnp.abs(rel)
    nf = jnp.maximum(n, 1).astype(jnp.float32)
    large = max_exact + (jnp.log(nf / max_exact) / math.log(REL_MAX_DIST / max_exact)
                         * (nb - max_exact)).astype(jnp.int32)
    large = jnp.minimum(large, nb - 1)
    return ret + jnp.where(n < max_exact, n, large)


def diff_attention(q, k, v, rel_table, lam, q_gain, k_gain, sub_gain, lam_init):
    B, S = q.shape[0], q.shape[1]
    n_qb = S // Q_BLOCK
    q = rmsnorm(q, q_gain) * (A_QK_DIM ** -0.5)
    k = rmsnorm(k, k_gain)
    kpos = jnp.arange(S, dtype=jnp.int32)
    q_blocks = jnp.moveaxis(q.reshape(B, n_qb, Q_BLOCK, A_HEADS, 2, A_QK_DIM), 1, 0)
    table = rel_table.astype(jnp.float32)

    def block(args):
        qi, i = args
        qpos = i * Q_BLOCK + jnp.arange(Q_BLOCK, dtype=jnp.int32)
        bias = table[t5_bucket(kpos[None, :] - qpos[:, None])]
        allowed = (kpos[None, :] // CHUNK) <= (qpos[:, None] // CHUNK)
        logits = jnp.einsum('bqhmd,bkhmd->bhmqk', qi, k, preferred_element_type=jnp.float32)
        logits = logits + jnp.transpose(bias, (2, 0, 1))[None, :, None]
        logits = jnp.where(allowed, logits, jnp.finfo(jnp.float32).min)
        p = jax.nn.softmax(logits, axis=-1)
        diff = p[:, :, 0] - lam * p[:, :, 1]
        return jnp.einsum('bhqk,bkhd->bqhd', diff, v.astype(jnp.float32))

    o = lax.map(block, (q_blocks, jnp.arange(n_qb, dtype=jnp.int32)))
    o = jnp.moveaxis(o, 0, 1).reshape(B, S, A_HEADS, A_V_DIM)
    o = rmsnorm(o, sub_gain) * (1.0 - lam_init)
    return o.reshape(B, S, A_WIDTH).astype(v.dtype)


def causal_conv(x, w, b):
    C = x.shape[-1]
    y = lax.conv_general_dilated(x, w[:, None, :].astype(x.dtype), window_strides=(1,),
                                 padding=[(CONV_WIDTH - 1, 0)],
                                 dimension_numbers=('NWC', 'WIO', 'NWC'),
                                 feature_group_count=C)
    return y + b.astype(x.dtype)


def rg_lru(x, w_a, b_a, w_x, b_x, lam_param):
    B, S, C = x.shape
    xf = x.astype(jnp.float32)
    xb = xf.reshape(B, S, LRU_BLOCKS, LRU_BLOCK_DIM)
    r = jax.nn.sigmoid(jnp.einsum('bsgi,gij->bsgj', xb, w_a.astype(jnp.float32)).reshape(B, S, C)
                       + b_a.astype(jnp.float32))
    i = jax.nn.sigmoid(jnp.einsum('bsgi,gij->bsgj', xb, w_x.astype(jnp.float32)).reshape(B, S, C)
                       + b_x.astype(jnp.float32))
    log_a = -LRU_C * r * jax.nn.softplus(-lam_param.astype(jnp.float32))
    a = jnp.exp(log_a)
    mult = jnp.sqrt(-jnp.expm1(2.0 * log_a))
    u = mult * (i * xf)

    def combine(c1, c2):
        a1, b1 = c1
        a2, b2 = c2
        return a1 * a2, a2 * b1 + b2

    _, h = lax.associative_scan(combine, (a, u), axis=1)
    return h.astype(x.dtype)


def moe(h, w_router, b_router, w1, b1, w2, b2):
    B, S, D = h.shape
    T = B * S
    xt = h.reshape(T, D)
    logits = (xt @ w_router + b_router).astype(jnp.float32)
    top_val, top_idx = lax.top_k(logits, TOP_K)
    gate = jax.nn.softmax(top_val, axis=-1)
    A = T * TOP_K
    e_flat = top_idx.reshape(A).astype(jnp.int32)
    tok_flat = jnp.arange(A, dtype=jnp.int32) // TOP_K
    g_flat = gate.reshape(A)
    order = jnp.argsort(e_flat, stable=True)
    e_sorted = e_flat[order]
    counts = jnp.bincount(e_flat, length=N_EXPERTS)
    starts = jnp.cumsum(counts) - counts
    padded = (counts + MOE_BLOCK - 1) // MOE_BLOCK * MOE_BLOCK
    pad_ends = jnp.cumsum(padded)
    pad_starts = pad_ends - padded
    dest = pad_starts[e_sorted] + (jnp.arange(A, dtype=jnp.int32) - starts[e_sorted])
    n_blocks = A // MOE_BLOCK + N_EXPERTS
    P = n_blocks * MOE_BLOCK
    buf_tok = jnp.zeros((P,), jnp.int32).at[dest].set(tok_flat[order])
    buf_gate = jnp.zeros((P,), jnp.float32).at[dest].set(g_flat[order])
    block_start = jnp.arange(n_blocks, dtype=jnp.int32) * MOE_BLOCK
    block_expert = jnp.minimum(jnp.searchsorted(pad_ends, block_start, side='right'),
                               N_EXPERTS - 1).astype(jnp.int32)

    def expert_block(args):
        tok, e = args
        hid = xt[tok] @ w1[e] + b1[e]
        x_glu = jnp.minimum(hid[:, ::2], SWIGLU_LIMIT)
        x_lin = jnp.clip(hid[:, 1::2], -SWIGLU_LIMIT, SWIGLU_LIMIT)
        act = x_glu * jax.nn.sigmoid(SWIGLU_ALPHA * x_glu) * (x_lin + 1.0)
        return act @ w2[e] + b2[e]

    out = lax.map(expert_block, (buf_tok.reshape(n_blocks, MOE_BLOCK), block_expert))
    out = out.reshape(P, D).astype(jnp.float32) * buf_gate[:, None]
    y = jnp.zeros((T, D), jnp.float32).at[buf_tok].add(out)
    return y.reshape(B, S, D).astype(h.dtype)


def setup_inputs(seed: int = 0) -> dict:
    key = jax.random.key(seed)
    ks = jax.random.split(key, 32)
    L, D = DEPTH, D_MODEL
    f32 = jnp.float32

    def nrm(k, shape, scale):
        return jax.random.normal(k, shape, f32) * scale

    def gain(k, shape):
        return 1.0 + 0.01 * jax.random.normal(k, shape, f32)

    a0 = jax.random.uniform(ks[17], (L, LRU_WIDTH), f32, 0.9, 0.999)
    s0 = a0 ** (1.0 / LRU_C)
    lru_lambda = jnp.log(s0) - jnp.log1p(-s0)
    return {
        'x': jax.random.normal(ks[0], (BATCH, SEQ, D), f32),
        'ln_mix': gain(ks[1], (L, D)),
        'w_in': nrm(ks[2], (L, D, IN_COLS), D ** -0.5),
        'b_gate': nrm(ks[3], (L, N_BRANCH * D), 0.01),
        'q_gain': gain(ks[4], (L, A_QK_DIM)),
        'k_gain': gain(ks[5], (L, A_QK_DIM)),
        'lambda_q1': nrm(ks[6], (L, A_QK_DIM), 0.1),
        'lambda_k1': nrm(ks[7], (L, A_QK_DIM), 0.1),
        'lambda_q2': nrm(ks[8], (L, A_QK_DIM), 0.1),
        'lambda_k2': nrm(ks[9], (L, A_QK_DIM), 0.1),
        'sub_gain': gain(ks[10], (L, A_V_DIM)),
        'rel_table': nrm(ks[11], (REL_BUCKETS, A_HEADS), 0.5),
        'conv_w': nrm(ks[12], (L, CONV_WIDTH, LRU_WIDTH), CONV_WIDTH ** -0.5),
        'conv_b': nrm(ks[13], (L, LRU_WIDTH), 0.01),
        'lru_wa': nrm(ks[14], (L, LRU_BLOCKS, LRU_BLOCK_DIM, LRU_BLOCK_DIM), LRU_BLOCK_DIM ** -0.5),
        'lru_ba': nrm(ks[15], (L, LRU_WIDTH), 0.01),
        'lru_wx': nrm(ks[16], (L, LRU_BLOCKS, LRU_BLOCK_DIM, LRU_BLOCK_DIM), LRU_BLOCK_DIM ** -0.5),
        'lru_bx': nrm(ks[18], (L, LRU_WIDTH), 0.01),
        'lru_lambda': lru_lambda,
        'w_branch_a': nrm(ks[19], (L, A_WIDTH, D), A_WIDTH ** -0.5),
        'w_branch_b': nrm(ks[20], (L, LRU_WIDTH, D), LRU_WIDTH ** -0.5),
        'w_out': nrm(ks[21], (L, D, D), D ** -0.5),
        'ln_ffn': gain(ks[22], (L, D)),
        'w_router': nrm(ks[23], (L, D, N_EXPERTS), D ** -0.5),
        'b_router': nrm(ks[24], (L, N_EXPERTS), 0.01),
        'w1': nrm(ks[25], (L, N_EXPERTS, D, 2 * D_EXPERT), D ** -0.5),
        'b1': nrm(ks[26], (L, N_EXPERTS, 2 * D_EXPERT), 0.01),
        'w2': nrm(ks[27], (L, N_EXPERTS, D_EXPERT, D), D_EXPERT ** -0.5),
        'b2': nrm(ks[28], (L, N_EXPERTS, D), 0.01),
    }


def reference(x, ln_mix, w_in, b_gate, q_gain, k_gain, lambda_q1, lambda_k1, lambda_q2, lambda_k2,
              sub_gain, rel_table, conv_w, conv_b, lru_wa, lru_ba, lru_wx, lru_bx, lru_lambda,
              w_branch_a, w_branch_b, w_out, ln_ffn, w_router, b_router, w1, b1, w2, b2):
    B, S, D = x.shape
    cuts = [Q_COLS, Q_COLS + K_COLS, Q_COLS + K_COLS + V_COLS,
            Q_COLS + K_COLS + V_COLS + LRU_WIDTH, Q_COLS + K_COLS + V_COLS + 2 * LRU_WIDTH]
    for l in range(DEPTH):
        h = rmsnorm(x, ln_mix[l])
        proj = jnp.einsum('bsd,dc->bsc', h, w_in[l])
        q, k, v, x_lru, g_lru, gate_logits = jnp.split(proj, cuts, axis=-1)
        q = q.reshape(B, S, A_HEADS, 2, A_QK_DIM)
        k = k.reshape(B, S, A_HEADS, 2, A_QK_DIM)
        v = v.reshape(B, S, A_HEADS, A_V_DIM)

        lam_init = 0.8 - 0.6 * math.exp(-0.3 * l)
        lam = (jnp.exp(jnp.sum(lambda_q1[l].astype(jnp.float32) * lambda_k1[l].astype(jnp.float32)))
               - jnp.exp(jnp.sum(lambda_q2[l].astype(jnp.float32) * lambda_k2[l].astype(jnp.float32)))
               + lam_init)
        o_a = diff_attention(q, k, v, rel_table, lam, q_gain[l], k_gain[l], sub_gain[l], lam_init)

        xr = causal_conv(x_lru, conv_w[l], conv_b[l])
        hr = rg_lru(xr, lru_wa[l], lru_ba[l], lru_wx[l], lru_bx[l], lru_lambda[l])
        o_b = hr * jax.nn.gelu(g_lru)

        y_a = jnp.einsum('bsc,cd->bsd', o_a, w_branch_a[l])
        y_b = jnp.einsum('bsc,cd->bsd', o_b, w_branch_b[l])
        g = jax.nn.sigmoid(gate_logits + b_gate[l]).reshape(B, S, N_BRANCH, D)
        mixed = g[:, :, 0] * y_a + g[:, :, 1] * y_b
        x = x + jnp.einsum('bsd,de->bse', mixed, w_out[l])

        x = x + moe(rmsnorm(x, ln_ffn[l]), w_router[l], b_router[l], w1[l], b1[l], w2[l], b2[l])
    return x
```

```python
import functools
import math

import jax
import jax.numpy as jnp
from jax import lax
from jax.experimental import pallas as pl
from jax.experimental.pallas import tpu as pltpu

F32 = jnp.float32
BF16 = jnp.bfloat16

CHUNK = 64
RMS_EPS = 1e-6
A_HEADS = 8
A_QK_DIM = 64
A_V_DIM = 2 * A_QK_DIM
LRU_BLOCKS = 8
CONV_WIDTH = 4
LRU_C = 8.0
REL_BUCKETS = 32
REL_MAX_DIST = 128
TOP_K = 4
SWIGLU_LIMIT = 7.0
SWIGLU_ALPHA = 1.702
LAM_INIT = 0.8 - 0.6 * math.exp(-0.3 * 0)

LANES = 128
SUBLANES = 8
NEG_BIG = -1e30
NEG_SEL = -3e38

ATT_BLOCK = 256
MOE_ROWS = 512
GATHER_ROWS = 256
COMBINE_TOKENS = 128


def _tile(n, pref):
    t = min(n, pref)
    assert n % t == 0, (n, pref)
    return t


def _params(semantics, vmem_mib):
    return pltpu.CompilerParams(dimension_semantics=semantics, vmem_limit_bytes=vmem_mib * 1024 * 1024)


def _rmsnorm_kernel(x_ref, g_ref, o_ref):
    x = x_ref[...]
    y = x * lax.rsqrt(jnp.mean(x * x, axis=-1, keepdims=True) + RMS_EPS)
    o_ref[...] = (y * g_ref[...]).astype(o_ref.dtype)


def _rmsnorm(x, g, out_dtype):
    m, d = x.shape
    tm = _tile(m, 512)
    return pl.pallas_call(
        _rmsnorm_kernel,
        grid=(m // tm,),
        in_specs=[pl.BlockSpec((tm, d), lambda i: (i, 0)), pl.BlockSpec((1, d), lambda i: (0, 0))],
        out_specs=pl.BlockSpec((tm, d), lambda i: (i, 0)),
        out_shape=jax.ShapeDtypeStruct((m, d), out_dtype),
        compiler_params=_params(("parallel",), 32),
        name="rmsnorm",
    )(x, g.reshape(1, d))


def _proj_kernel(a_ref, w_ref, *rest, sigmoid_bias):
    if sigmoid_bias:
        b_ref, o_ref = rest
    else:
        (o_ref,) = rest
    acc = jnp.dot(a_ref[...], w_ref[...], preferred_element_type=F32)
    if sigmoid_bias:
        acc = jax.nn.sigmoid(acc + b_ref[...])
    o_ref[...] = acc.astype(o_ref.dtype)


def _proj(a, w, out_dtype, bias=None, tm=1024, tn=1024):
    m, k = a.shape
    n = w.shape[1]
    tm, tn = _tile(m, tm), _tile(n, tn)
    in_specs = [pl.BlockSpec((tm, k), lambda i, j: (i, 0)), pl.BlockSpec((k, tn), lambda i, j: (0, j))]
    args = [a, w]
    if bias is not None:
        in_specs.append(pl.BlockSpec((1, tn), lambda i, j: (0, j)))
        args.append(bias.reshape(1, n))
    return pl.pallas_call(
        functools.partial(_proj_kernel, sigmoid_bias=bias is not None),
        grid=(m // tm, n // tn),
        in_specs=in_specs,
        out_specs=pl.BlockSpec((tm, tn), lambda i, j: (i, j)),
        out_shape=jax.ShapeDtypeStruct((m, n), out_dtype),
        compiler_params=_params(("parallel", "parallel"), 48),
        name="proj",
    )(*args)


def _t5_bucket(rel):
    nb = REL_BUCKETS // 2
    max_exact = nb // 2
    ret = jnp.where(rel > 0, nb, 0)
    n = jnp.abs(rel)
    nf = jnp.maximum(n, 1).astype(F32)
    large = max_exact + (jnp.log(nf / max_exact) / math.log(REL_MAX_DIST / max_exact)
                         * (nb - max_exact)).astype(jnp.int32)
    large = jnp.minimum(large, nb - 1)
    return ret + jnp.where(n < max_exact, n, large)


def _bias_tiles(rel_table, t):
    table = rel_table.astype(F32)
    r = jnp.arange(t, dtype=jnp.int32)[:, None]
    c = jnp.arange(t, dtype=jnp.int32)[None, :]
    diag = jnp.transpose(table[_t5_bucket(c - r)], (2, 0, 1))
    allowed = (c // CHUNK) <= (r // CHUNK)
    diag = jnp.where(allowed[None], diag, NEG_BIG)
    sub = jnp.transpose(table[_t5_bucket(c - r - t)], (2, 0, 1))
    far = table[_t5_bucket(jnp.int32(-(t + 1)))]
    return diag, sub, far


def _attn_kernel(scal_ref, q_ref, k_ref, v_ref, bd_ref, bs_ref, qg_ref, kg_ref, sg_ref, o_ref,
                 kn_ref, m_ref, l_ref, acc_ref, *, t, s_len, k_chunk):
    h = pl.program_id(1)
    qi = pl.program_id(2)
    lo = lax.broadcasted_iota(jnp.int32, (1, 2 * A_QK_DIM), 1) < A_QK_DIM

    def qk_norm(x, g):
        sq = x * x
        s_lo = jnp.sum(jnp.where(lo, sq, 0.0), axis=-1, keepdims=True)
        s_hi = jnp.sum(jnp.where(lo, 0.0, sq), axis=-1, keepdims=True)
        ms = jnp.where(lo, s_lo, s_hi) * (1.0 / A_QK_DIM)
        return x * lax.rsqrt(ms + RMS_EPS) * g

    @pl.when(qi == 0)
    def _():
        def body(c, carry):
            r0 = pl.multiple_of(c * k_chunk, k_chunk)
            kk = k_ref[0, pl.ds(r0, k_chunk), :].astype(F32)
            kn_ref[pl.ds(r0, k_chunk), :] = qk_norm(kk, kg_ref[...]).astype(BF16)
            return carry
        lax.fori_loop(0, s_len // k_chunk, body, 0)

    q = qk_norm(q_ref[0].astype(F32), qg_ref[...]) * (A_QK_DIM ** -0.5)
    qs = jnp.concatenate([jnp.where(lo, q, 0.0), jnp.where(lo, 0.0, q)], axis=0).astype(BF16)

    m_ref[...] = jnp.full(m_ref.shape, NEG_BIG, F32)
    l_ref[...] = jnp.zeros(l_ref.shape, F32)
    acc_ref[...] = jnp.zeros(acc_ref.shape, F32)

    def step(j, bias):
        r0 = pl.multiple_of(j * t, t)
        kj = kn_ref[pl.ds(r0, t), :]
        vj = v_ref[0, pl.ds(r0, t), :]
        s = lax.dot_general(qs, kj, (((1,), (1,)), ((), ())), preferred_element_type=F32)
        if bias.ndim == 2:
            s = (s.reshape(2, t, t) + bias[None]).reshape(2 * t, t)
        else:
            s = s + bias
        m_prev = m_ref[...]
        m_new = jnp.maximum(m_prev, jnp.max(s, axis=-1, keepdims=True))
        alpha = jnp.exp(m_prev - m_new)
        p = jnp.exp(s - m_new)
        l_ref[...] = alpha * l_ref[...] + jnp.sum(p, axis=-1, keepdims=True)
        acc_ref[...] = alpha * acc_ref[...] + jnp.dot(p.astype(BF16), vj, preferred_element_type=F32)
        m_ref[...] = m_new

    far = scal_ref[h]

    def far_body(j, carry):
        step(j, far)
        return carry
    lax.fori_loop(0, jnp.maximum(qi - 1, 0), far_body, 0)

    @pl.when(qi >= 1)
    def _():
        step(qi - 1, bs_ref[0])

    step(qi, bd_ref[0])

    lam = scal_ref[A_HEADS]
    acc = acc_ref[...]
    l = l_ref[...]
    o = acc[:t] / l[:t] - lam * (acc[t:] / l[t:])
    o = o * lax.rsqrt(jnp.mean(o * o, axis=-1, keepdims=True) + RMS_EPS) * sg_ref[...]
    o_ref[0] = (o * (1.0 - LAM_INIT)).astype(o_ref.dtype)


def _diff_attention(qkv, rel_table, lam, q_gain, k_gain, sub_gain):
    b, s_len, _ = qkv.shape
    t = _tile(s_len, ATT_BLOCK)
    hw = 2 * A_QK_DIM
    diag, sub, far = _bias_tiles(rel_table, t)
    scal = jnp.concatenate([far, lam.reshape(1)]).astype(F32)
    tile2 = lambda g: jnp.concatenate([g, g]).reshape(1, hw).astype(F32)
    kern = functools.partial(_attn_kernel, t=t, s_len=s_len, k_chunk=_tile(s_len, 512))
    return pl.pallas_call(
        kern,
        grid=(b, A_HEADS, s_len // t),
        in_specs=[
            pl.BlockSpec(memory_space=pltpu.SMEM),
            pl.BlockSpec((1, t, hw), lambda bi, h, qi: (bi, qi, h)),
            pl.BlockSpec((1, s_len, hw), lambda bi, h, qi: (bi, 0, A_HEADS + h)),
            pl.BlockSpec((1, s_len, hw), lambda bi, h, qi: (bi, 0, 2 * A_HEADS + h)),
            pl.BlockSpec((1, t, t), lambda bi, h, qi: (h, 0, 0)),
            pl.BlockSpec((1, t, t), lambda bi, h, qi: (h, 0, 0)),
            pl.BlockSpec((1, hw), lambda bi, h, qi: (0, 0)),
            pl.BlockSpec((1, hw), lambda bi, h, qi: (0, 0)),
            pl.BlockSpec((1, hw), lambda bi, h, qi: (0, 0)),
        ],
        out_specs=pl.BlockSpec((1, t, hw), lambda bi, h, qi: (bi, qi, h)),
        out_shape=jax.ShapeDtypeStruct((b, s_len, A_HEADS * hw), BF16),
        scratch_shapes=[
            pltpu.VMEM((s_len, hw), BF16),
            pltpu.VMEM((2 * t, 1), F32),
            pltpu.VMEM((2 * t, 1), F32),
            pltpu.VMEM((2 * t, hw), F32),
        ],
        compiler_params=_params(("arbitrary", "arbitrary", "arbitrary"), 40),
        name="diff_attention",
    )(scal, qkv, qkv, qkv, diag, sub, tile2(q_gain), tile2(k_gain), sub_gain.reshape(1, hw).astype(F32))


def _lru_kernel(xl_ref, gl_ref, cw_ref, cb_ref, wa_ref, ba_ref, wx_ref, bx_ref, c_ref, o_ref,
                xbuf, a_s, u_s, h_s, hc, *, ts, width):
    i = pl.program_id(1)
    halo = SUBLANES
    bd = width // LRU_BLOCKS

    @pl.when(i == 0)
    def _():
        xbuf[0:halo, :] = jnp.zeros((halo, width), F32)
        hc[...] = jnp.zeros(hc.shape, F32)

    x = xl_ref[0]
    xbuf[halo:halo + ts, :] = x
    xr = cb_ref[...] + cw_ref[0:1, :] * xbuf[pl.ds(halo - 3, ts), :]
    for j in range(1, CONV_WIDTH):
        xr = xr + cw_ref[j:j + 1, :] * xbuf[pl.ds(halo - 3 + j, ts), :]
    xbuf[0:halo, :] = x[ts - halo:, :]

    for g in range(LRU_BLOCKS):
        sl = slice(g * bd, (g + 1) * bd)
        xg = xr[:, sl]
        xb = xg.astype(BF16)
        r = jax.nn.sigmoid(jnp.dot(xb, wa_ref[g], preferred_element_type=F32) + ba_ref[:, sl])
        gi = jax.nn.sigmoid(jnp.dot(xb, wx_ref[g], preferred_element_type=F32) + bx_ref[:, sl])
        log_a = r * c_ref[:, sl]
        a = jnp.exp(log_a)
        a_s[:, sl] = a
        u_s[:, sl] = jnp.sqrt(-jnp.tanh(log_a) * (1.0 + a * a)) * (gi * xg)

    row = lax.broadcasted_iota(jnp.int32, (SUBLANES, width), 0)

    def body(gidx, h):
        r0 = pl.multiple_of(gidx * SUBLANES, SUBLANES)
        a = a_s[pl.ds(r0, SUBLANES), :]
        u = u_s[pl.ds(r0, SUBLANES), :]
        for d in (1, 2, 4):
            keep = row >= d
            u = jnp.where(keep, a * pltpu.roll(u, d, 0) + u, u)
            a = jnp.where(keep, a * pltpu.roll(a, d, 0), a)
        hr = a * h + u
        h_s[pl.ds(r0, SUBLANES), :] = hr
        return hr[SUBLANES - 1:SUBLANES, :]

    hc[...] = lax.fori_loop(0, ts // SUBLANES, body, hc[...], unroll=2)
    o_ref[0] = (h_s[...] * jax.nn.gelu(gl_ref[0])).astype(o_ref.dtype)


def _lru_branch(lru, conv_w, conv_b, wa, ba, wx, bx, lam_param):
    b, s_len, c2 = lru.shape
    width = c2 // 2
    ts = _tile(s_len, 256)
    bd = width // LRU_BLOCKS
    c_vec = (-LRU_C * jax.nn.softplus(-lam_param.astype(F32))).reshape(1, width)
    row = lambda v: v.reshape(1, width).astype(F32)
    const2 = lambda shape: pl.BlockSpec(shape, lambda bi, i: (0,) * len(shape))
    kern = functools.partial(_lru_kernel, ts=ts, width=width)
    return pl.pallas_call(
        kern,
        grid=(b, s_len // ts),
        in_specs=[
            pl.BlockSpec((1, ts, width), lambda bi, i: (bi, i, 0)),
            pl.BlockSpec((1, ts, width), lambda bi, i: (bi, i, 1)),
            const2((CONV_WIDTH, width)), const2((1, width)),
            const2((LRU_BLOCKS, bd, bd)), const2((1, width)),
            const2((LRU_BLOCKS, bd, bd)), const2((1, width)),
            const2((1, width)),
        ],
        out_specs=pl.BlockSpec((1, ts, width), lambda bi, i: (bi, i, 0)),
        out_shape=jax.ShapeDtypeStruct((b, s_len, width), BF16),
        scratch_shapes=[
            pltpu.VMEM((ts + SUBLANES, width), F32),
            pltpu.VMEM((ts, width), F32),
            pltpu.VMEM((ts, width), F32),
            pltpu.VMEM((ts, width), F32),
            pltpu.VMEM((1, width), F32),
        ],
        compiler_params=_params(("arbitrary", "arbitrary"), 40),
        name="rg_lru",
    )(lru, lru, conv_w.astype(F32), row(conv_b), wa.astype(BF16), row(ba), wx.astype(BF16), row(bx), c_vec)


def _merge_kernel(oa_ref, ob_ref, wa_ref, wb_ref, g0_ref, g1_ref, o_ref):
    ya = jnp.dot(oa_ref[...], wa_ref[...], preferred_element_type=F32)
    yb = jnp.dot(ob_ref[...], wb_ref[...], preferred_element_type=F32)
    o_ref[...] = (g0_ref[...].astype(F32) * ya + g1_ref[...].astype(F32) * yb).astype(o_ref.dtype)


def _merge(o_a, o_b, w_a, w_b, gates):
    m, ka = o_a.shape
    kb = o_b.shape[1]
    d = w_a.shape[1]
    tm, tn = _tile(m, 512), _tile(d, 1024)
    nj = d // tn
    return pl.pallas_call(
        _merge_kernel,
        grid=(m // tm, nj),
        in_specs=[
            pl.BlockSpec((tm, ka), lambda i, j: (i, 0)),
            pl.BlockSpec((tm, kb), lambda i, j: (i, 0)),
            pl.BlockSpec((ka, tn), lambda i, j: (0, j)),
            pl.BlockSpec((kb, tn), lambda i, j: (0, j)),
            pl.BlockSpec((tm, tn), lambda i, j: (i, j)),
            pl.BlockSpec((tm, tn), lambda i, j: (i, j + nj)),
        ],
        out_specs=pl.BlockSpec((tm, tn), lambda i, j: (i, j)),
        out_shape=jax.ShapeDtypeStruct((m, d), BF16),
        compiler_params=_params(("parallel", "parallel"), 40),
        name="branch_merge",
    )(o_a, o_b, w_a, w_b, gates, gates)


def _outproj_kernel(mx_ref, wo_ref, x_ref, g_ref, wr_ref, br_ref,
                    x1_ref, xn_ref, ti_ref, tg_ref, pos_ref, cnt_ref, carry, *, tm):
    i = pl.program_id(0)

    @pl.when(i == 0)
    def _():
        carry[...] = jnp.zeros(carry.shape, F32)

    x1 = x_ref[...] + jnp.dot(mx_ref[...], wo_ref[...], preferred_element_type=F32)
    x1_ref[...] = x1
    xn = x1 * lax.rsqrt(jnp.mean(x1 * x1, axis=-1, keepdims=True) + RMS_EPS) * g_ref[...]
    xn_ref[...] = xn
    logits = jnp.dot(xn, wr_ref[...], preferred_element_type=F32, precision=lax.Precision.HIGHEST) + br_ref[...]

    lane = lax.broadcasted_iota(jnp.int32, (tm, LANES), 1)
    rest = logits
    vals, idxs = [], []
    for _ in range(TOP_K):
        mx = jnp.max(rest, axis=-1, keepdims=True)
        ix = jnp.min(jnp.where(rest == mx, lane, LANES), axis=-1, keepdims=True)
        vals.append(mx)
        idxs.append(ix)
        rest = jnp.where(lane == ix, NEG_SEL, rest)
    exps = [jnp.exp(v - vals[0]) for v in vals]
    den = exps[0]
    for e in exps[1:]:
        den = den + e
    ti = jnp.zeros((tm, LANES), jnp.int32)
    tg = jnp.zeros((tm, LANES), F32)
    sel = jnp.zeros((tm, LANES), F32)
    for k in range(TOP_K):
        ti = jnp.where(lane == k, idxs[k], ti)
        tg = jnp.where(lane == k, exps[k] / den, tg)
        sel = jnp.where(lane == idxs[k], 1.0, sel)
    ti_ref[...] = ti
    tg_ref[...] = tg

    rr = lax.broadcasted_iota(jnp.int32, (tm, tm), 0)
    cc = lax.broadcasted_iota(jnp.int32, (tm, tm), 1)
    below = (cc < rr).astype(BF16)
    pos = jnp.dot(below, sel.astype(BF16), preferred_element_type=F32) + carry[...]
    pos_ref[...] = pos.astype(jnp.int32)
    carry[...] = carry[...] + jnp.sum(sel, axis=0, keepdims=True)
    cnt_ref[...] = carry[...].astype(jnp.int32)


def _outproj_router(mixed, w_out, x, ln_ffn, w_router, b_router):
    m, d = x.shape
    e = w_router.shape[1]
    assert e <= LANES
    tm = _tile(m, 256)
    wr = jnp.zeros((d, LANES), F32).at[:, :e].set(w_router.astype(F32))
    br = jnp.full((1, LANES), NEG_BIG, F32).at[0, :e].set(b_router.astype(F32))
    row_blk = lambda w: pl.BlockSpec((tm, w), lambda i: (i, 0))
    const = lambda shape: pl.BlockSpec(shape, lambda i: (0, 0))
    return pl.pallas_call(
        functools.partial(_outproj_kernel, tm=tm),
        grid=(m // tm,),
        in_specs=[row_blk(d), const((d, d)), row_blk(d), const((1, d)), const((d, LANES)), const((1, LANES))],
        out_specs=[row_blk(d), row_blk(d), row_blk(LANES), row_blk(LANES), row_blk(LANES), const((1, LANES))],
        out_shape=[
            jax.ShapeDtypeStruct((m, d), F32),
            jax.ShapeDtypeStruct((m, d), F32),
            jax.ShapeDtypeStruct((m, LANES), jnp.int32),
            jax.ShapeDtypeStruct((m, LANES), F32),
            jax.ShapeDtypeStruct((m, LANES), jnp.int32),
            jax.ShapeDtypeStruct((1, LANES), jnp.int32),
        ],
        scratch_shapes=[pltpu.VMEM((1, LANES), F32)],
        compiler_params=_params(("arbitrary",), 48),
        name="outproj_router",
    )(mixed, w_out, x, ln_ffn.reshape(1, d).astype(F32), wr, br)


def _gather_kernel(idx_ref, nxt_ref, src_ref, *rest, rows, groups, weighted):
    if weighted:
        g_ref, base_ref, o_ref, buf, sem = rest
    else:
        o_ref, buf, sem = rest
    i = pl.program_id(0)
    n = pl.num_programs(0)
    total = rows * groups

    def row_copy(ref, r, slot):
        return pltpu.make_async_copy(src_ref.at[pl.ds(ref[0, 0, r], 1), :],
                                     buf.at[slot, pl.ds(r, 1), :], sem.at[slot])

    def issue(ref, slot):
        def body(r, carry):
            row_copy(ref, r, slot).start()
            return carry
        lax.fori_loop(0, total, body, 0, unroll=8)

    @pl.when(i == 0)
    def _():
        issue(idx_ref, 0)

    @pl.when(i + 1 < n)
    def _():
        issue(nxt_ref, (i + 1) % 2)

    slot = i % 2

    def wait_body(r, carry):
        row_copy(idx_ref, r, slot).wait()
        return carry
    lax.fori_loop(0, total, wait_body, 0, unroll=8)

    if weighted:
        lane = lax.broadcasted_iota(jnp.int32, (rows, LANES), 1)
        g = g_ref[...]
        acc = base_ref[...]
        for k in range(groups):
            gk = jnp.sum(jnp.where(lane == k, g, 0.0), axis=-1, keepdims=True)
            acc = acc + gk * buf[slot, pl.ds(k * rows, rows), :]
        o_ref[...] = acc.astype(o_ref.dtype)
    else:
        o_ref[...] = buf[slot].astype(o_ref.dtype)


def _gather_rows(src, idx, rows, out_dtype, gates=None, base=None):
    steps, _, total = idx.shape
    groups = total // rows
    d = src.shape[1]
    weighted = gates is not None
    in_specs = [
        pl.BlockSpec((1, 1, total), lambda i: (i, 0, 0), memory_space=pltpu.SMEM),
        pl.BlockSpec((1, 1, total), lambda i: (jnp.minimum(i + 1, steps - 1), 0, 0), memory_space=pltpu.SMEM),
        pl.BlockSpec(memory_space=pl.ANY),
    ]
    args = [idx, idx, src]
    if weighted:
        in_specs += [pl.BlockSpec((rows, LANES), lambda i: (i, 0)), pl.BlockSpec((rows, d), lambda i: (i, 0))]
        args += [gates, base]
    return pl.pallas_call(
        functools.partial(_gather_kernel, rows=rows, groups=groups, weighted=weighted),
        grid=(steps,),
        in_specs=in_specs,
        out_specs=pl.BlockSpec((rows, d), lambda i: (i, 0)),
        out_shape=jax.ShapeDtypeStruct((steps * rows, d), out_dtype),
        scratch_shapes=[pltpu.VMEM((2, total, d), F32), pltpu.SemaphoreType.DMA((2,))],
        compiler_params=_params(("arbitrary",), 48),
        name="combine_rows" if weighted else "gather_rows",
    )(*args)


def _zero_unused_block(nu_ref, o_ref):
    @pl.when(pl.program_id(1) >= nu_ref[0])
    def _():
        o_ref[...] = jnp.zeros(o_ref.shape, o_ref.dtype)


def _moe1_kernel(be_ref, nu_ref, x_ref, wg_ref, wl_ref, bg_ref, bl_ref, o_ref):
    _zero_unused_block(nu_ref, o_ref)

    @pl.when(pl.program_id(1) < nu_ref[0])
    def _():
        x = x_ref[...]
        hg = jnp.dot(x, wg_ref[0], preferred_element_type=F32) + bg_ref[0]
        hl = jnp.dot(x, wl_ref[0], preferred_element_type=F32) + bl_ref[0]
        glu = jnp.minimum(hg, SWIGLU_LIMIT)
        lin = jnp.clip(hl, -SWIGLU_LIMIT, SWIGLU_LIMIT)
        o_ref[...] = (glu * jax.nn.sigmoid(SWIGLU_ALPHA * glu) * (lin + 1.0)).astype(o_ref.dtype)


def _moe2_kernel(be_ref, nu_ref, a_ref, w_ref, b_ref, o_ref):
    _zero_unused_block(nu_ref, o_ref)

    @pl.when(pl.program_id(1) < nu_ref[0])
    def _():
        o_ref[...] = jnp.dot(a_ref[...], w_ref[0], preferred_element_type=F32) + b_ref[0]


def _grouped(kernel, name, rows_in, weights, biases, blk_expert, n_used, bm, tn, out_dtype):
    p, k = rows_in.shape
    n = weights[0].shape[2]
    tn = _tile(n, tn)
    nblk = p // bm
    blk = lambda i, nu: jnp.minimum(i, nu[0] - 1)
    w_spec = pl.BlockSpec((1, k, tn), lambda j, i, be, nu: (be[i], 0, j))
    b_spec = pl.BlockSpec((1, 1, tn), lambda j, i, be, nu: (be[i], 0, j))
    grid_spec = pltpu.PrefetchScalarGridSpec(
        num_scalar_prefetch=2,
        grid=(n // tn, nblk),
        in_specs=[pl.BlockSpec((bm, k), lambda j, i, be, nu: (blk(i, nu), 0))]
        + [w_spec] * len(weights) + [b_spec] * len(biases),
        out_specs=pl.BlockSpec((bm, tn), lambda j, i, be, nu: (i, j)),
    )
    return pl.pallas_call(
        kernel,
        grid_spec=grid_spec,
        out_shape=jax.ShapeDtypeStruct((p, n), out_dtype),
        compiler_params=_params(("arbitrary", "arbitrary"), 48),
        name=name,
    )(blk_expert, n_used, rows_in, *weights, *biases)


def _moe(x1, xn, ti, tg, pos, cnt, w1, b1, w2, b2):
    t_tok, d = x1.shape
    n_exp, _, f2 = w1.shape
    f = f2 // 2
    bm = MOE_ROWS
    assert (t_tok * TOP_K) % bm == 0
    nblk = t_tok * TOP_K // bm + n_exp
    p = nblk * bm

    ti4 = ti[:, :TOP_K]
    counts = cnt[0, :n_exp]
    padded = (counts + bm - 1) // bm * bm
    pad_ends = jnp.cumsum(padded)
    pad_starts = pad_ends - padded
    dest4 = pad_starts[ti4] + jnp.take_along_axis(pos, ti4, axis=1)
    n_used = (pad_ends[-1] // bm).astype(jnp.int32).reshape(1)
    blk_start = jnp.arange(nblk, dtype=jnp.int32) * bm
    blk_expert = jnp.minimum(jnp.searchsorted(pad_ends, blk_start, side='right'), n_exp - 1).astype(jnp.int32)
    blk_expert = jnp.where(jnp.arange(nblk) < n_used[0], blk_expert, blk_expert[jnp.maximum(n_used[0] - 1, 0)])
    tok = jnp.broadcast_to(jnp.arange(t_tok, dtype=jnp.int32)[:, None], (t_tok, TOP_K))
    row_tok = jnp.zeros((p,), jnp.int32).at[dest4.reshape(-1)].set(tok.reshape(-1), unique_indices=True)

    gr = _tile(p, GATHER_ROWS)
    xg = _gather_rows(xn, row_tok.reshape(p // gr, 1, gr), gr, BF16)

    w1g = w1[:, :, 0::2].astype(BF16)
    w1l = w1[:, :, 1::2].astype(BF16)
    b1g = b1[:, 0::2].reshape(n_exp, 1, f).astype(F32)
    b1l = b1[:, 1::2].reshape(n_exp, 1, f).astype(F32)
    act = _grouped(_moe1_kernel, "expert_up", xg, [w1g, w1l], [b1g, b1l], blk_expert, n_used, bm, 512, BF16)
    out = _grouped(_moe2_kernel, "expert_down", act, [w2.astype(BF16)], [b2.reshape(n_exp, 1, d).astype(F32)],
                   blk_expert, n_used, bm, 1024, F32)

    tc = _tile(t_tok, COMBINE_TOKENS)
    steps = t_tok // tc
    idx = jnp.transpose(dest4.reshape(steps, tc, TOP_K), (0, 2, 1)).reshape(steps, 1, TOP_K * tc)
    return _gather_rows(out, idx, tc, F32, gates=tg, base=x1)


def kernel(x, ln_mix, w_in, b_gate, q_gain, k_gain, lambda_q1, lambda_k1, lambda_q2, lambda_k2, sub_gain, rel_table, conv_w, conv_b, lru_wa, lru_ba, lru_wx, lru_bx, lru_lambda, w_branch_a, w_branch_b, w_out, ln_ffn, w_router, b_router, w1, b1, w2, b2):
    b, s_len, d = x.shape
    t_tok = b * s_len
    depth = ln_mix.shape[0]
    assert depth == 1
    q_cols = A_HEADS * 2 * A_QK_DIM
    qkv_cols = 2 * q_cols + A_HEADS * A_V_DIM
    lru_w = d // 2
    xt = x.reshape(t_tok, d)
    l = 0

    hn = _rmsnorm(xt, ln_mix[l], BF16)
    w_in_b = w_in[l].astype(BF16)
    qkv = _proj(hn, w_in_b[:, :qkv_cols], BF16)
    lru = _proj(hn, w_in_b[:, qkv_cols:qkv_cols + 2 * lru_w], F32)
    gates = _proj(hn, w_in_b[:, qkv_cols + 2 * lru_w:], BF16, bias=b_gate[l])

    lam = (jnp.exp(jnp.sum(lambda_q1[l].astype(F32) * lambda_k1[l].astype(F32)))
           - jnp.exp(jnp.sum(lambda_q2[l].astype(F32) * lambda_k2[l].astype(F32))) + LAM_INIT)
    o_a = _diff_attention(qkv.reshape(b, s_len, qkv_cols), rel_table, lam, q_gain[l], k_gain[l], sub_gain[l])
    o_b = _lru_branch(lru.reshape(b, s_len, 2 * lru_w), conv_w[l], conv_b[l], lru_wa[l], lru_ba[l],
                      lru_wx[l], lru_bx[l], lru_lambda[l])

    mixed = _merge(o_a.reshape(t_tok, -1), o_b.reshape(t_tok, lru_w),
                   w_branch_a[l].astype(BF16), w_branch_b[l].astype(BF16), gates)
    x1, xn, ti, tg, pos, cnt = _outproj_router(mixed, w_out[l].astype(BF16), xt, ln_ffn[l], w_router[l], b_router[l])
    y = _moe(x1, xn, ti, tg, pos, cnt, w1[l], b1[l], w2[l], b2[l])
    return y.reshape(b, s_len, d)
```

```python
import functools
import math

import jax
import jax.numpy as jnp
from jax import lax
from jax.experimental import pallas as pl
from jax.experimental.pallas import tpu as pltpu

F32 = jnp.float32
BF16 = jnp.bfloat16

CHUNK = 64
RMS_EPS = 1e-6
A_HEADS = 8
A_QK_DIM = 64
A_V_DIM = 2 * A_QK_DIM
LRU_BLOCKS = 8
CONV_WIDTH = 4
LRU_C = 8.0
REL_BUCKETS = 32
REL_MAX_DIST = 128
TOP_K = 4
SWIGLU_LIMIT = 7.0
SWIGLU_ALPHA = 1.702
LAM_INIT = 0.8 - 0.6 * math.exp(-0.3 * 0)
LOG2E = 1.4426950408889634

LANES = 128
SUBLANES = 8
MXU_DIM = 256
NEG_BIG = -1e30
NEG_SEL = -3e38

ATT_Q_BLOCK = 512
MOE_ROWS = 512
GATHER_ROWS = 256
COMBINE_TOKENS = 128


def _tile(n, pref):
    t = min(n, pref)
    assert n % t == 0, (n, pref)
    return t


def _params(semantics, vmem_mib):
    return pltpu.CompilerParams(dimension_semantics=semantics, vmem_limit_bytes=vmem_mib * 1024 * 1024)


def _rmsnorm_kernel(x_ref, g_ref, o_ref):
    x = x_ref[...]
    y = x * lax.rsqrt(jnp.mean(x * x, axis=-1, keepdims=True) + RMS_EPS)
    o_ref[...] = (y * g_ref[...]).astype(o_ref.dtype)


def _rmsnorm(x, g, out_dtype):
    m, d = x.shape
    tm = _tile(m, 512)
    return pl.pallas_call(
        _rmsnorm_kernel,
        grid=(m // tm,),
        in_specs=[pl.BlockSpec((tm, d), lambda i: (i, 0)), pl.BlockSpec((1, d), lambda i: (0, 0))],
        out_specs=pl.BlockSpec((tm, d), lambda i: (i, 0)),
        out_shape=jax.ShapeDtypeStruct((m, d), out_dtype),
        compiler_params=_params(("parallel",), 32),
        name="rmsnorm",
    )(x, g.reshape(1, d))


def _proj_kernel(a_ref, w_ref, *rest, sigmoid_bias):
    if sigmoid_bias:
        b_ref, o_ref = rest
    else:
        (o_ref,) = rest
    acc = jnp.dot(a_ref[...], w_ref[...], preferred_element_type=F32)
    if sigmoid_bias:
        acc = jax.nn.sigmoid(acc + b_ref[...])
    o_ref[...] = acc.astype(o_ref.dtype)


def _proj(a, w, out_dtype, bias=None, tm=1024, tn=1024):
    m, k = a.shape
    n = w.shape[1]
    tm, tn = _tile(m, tm), _tile(n, tn)
    in_specs = [pl.BlockSpec((tm, k), lambda i, j: (i, 0)), pl.BlockSpec((k, tn), lambda i, j: (0, j))]
    args = [a, w]
    if bias is not None:
        in_specs.append(pl.BlockSpec((1, tn), lambda i, j: (0, j)))
        args.append(bias.reshape(1, n))
    return pl.pallas_call(
        functools.partial(_proj_kernel, sigmoid_bias=bias is not None),
        grid=(m // tm, n // tn),
        in_specs=in_specs,
        out_specs=pl.BlockSpec((tm, tn), lambda i, j: (i, j)),
        out_shape=jax.ShapeDtypeStruct((m, n), out_dtype),
        compiler_params=_params(("parallel", "parallel"), 48),
        name="proj",
    )(*args)


def _t5_bucket(rel):
    nb = REL_BUCKETS // 2
    max_exact = nb // 2
    ret = jnp.where(rel > 0, nb, 0)
    n = jnp.abs(rel)
    nf = jnp.maximum(n, 1).astype(F32)
    large = max_exact + (jnp.log(nf / max_exact) / math.log(REL_MAX_DIST / max_exact)
                         * (nb - max_exact)).astype(jnp.int32)
    large = jnp.minimum(large, nb - 1)
    return ret + jnp.where(n < max_exact, n, large)


def _near_bias(rel_table, t):
    assert t + 1 >= REL_MAX_DIST
    table = rel_table.astype(F32)
    r = jnp.arange(t, dtype=jnp.int32)[:, None]
    c = jnp.arange(t, dtype=jnp.int32)[None, :]
    far = table[_t5_bucket(jnp.int32(-(t + 1)))][:, None, None]
    diag = jnp.transpose(table[_t5_bucket(c - r)], (2, 0, 1)) - far
    allowed = (c // CHUNK) <= (r // CHUNK)
    diag = jnp.where(allowed[None], diag * LOG2E, NEG_BIG)
    sub = (jnp.transpose(table[_t5_bucket(c - r - t)], (2, 0, 1)) - far) * LOG2E
    future = jnp.full_like(diag, NEG_BIG)
    first = jnp.concatenate([diag, future], axis=-1)
    later = jnp.concatenate([sub, diag], axis=-1)
    return jnp.stack([first, later], axis=1)


def _attn_kernel(lam_ref, q_ref, k_ref, v_ref, nb_ref, qg_ref, kg_ref, sg_ref, o_ref,
                 kn_ref, vx_ref, m_ref, acc_ref, *, t, s_len, k_chunk):
    qi = pl.program_id(2)
    hw = 2 * A_QK_DIM
    rows = 2 * t
    lo = lax.broadcasted_iota(jnp.int32, (1, hw), 1) < A_QK_DIM

    def qk_norm(x, g):
        sq = x * x
        s_lo = jnp.sum(jnp.where(lo, sq, 0.0), axis=-1, keepdims=True)
        s_hi = jnp.sum(jnp.where(lo, 0.0, sq), axis=-1, keepdims=True)
        ms = jnp.where(lo, s_lo, s_hi) * (1.0 / A_QK_DIM)
        return x * lax.rsqrt(ms + RMS_EPS) * g

    @pl.when(qi == 0)
    def _():
        def body(c, carry):
            r0 = pl.multiple_of(c * k_chunk, k_chunk)
            kk = k_ref[0, pl.ds(r0, k_chunk), :].astype(F32)
            kn_ref[pl.ds(r0, k_chunk), :] = qk_norm(kk, kg_ref[...]).astype(BF16)
            vx_ref[pl.ds(r0, k_chunk), 0:hw] = v_ref[0, pl.ds(r0, k_chunk), :]
            vx_ref[pl.ds(r0, k_chunk), hw:2 * hw] = jnp.ones((k_chunk, hw), BF16)
            return carry
        lax.fori_loop(0, s_len // k_chunk, body, 0)

    q = qk_norm(q_ref[0].astype(F32), qg_ref[...]) * (A_QK_DIM ** -0.5 * LOG2E)
    qs = jnp.concatenate([jnp.where(lo, q, 0.0), jnp.where(lo, 0.0, q)], axis=0).astype(BF16)

    m_ref[...] = jnp.full(m_ref.shape, NEG_BIG, F32)
    acc_ref[...] = jnp.zeros(acc_ref.shape, F32)

    def step(r0, width, bias):
        kj = kn_ref[pl.ds(r0, width), :]
        vj = vx_ref[pl.ds(r0, width), :]
        s = lax.dot_general(qs, kj, (((1,), (1,)), ((), ())), preferred_element_type=F32)
        if bias is not None:
            s = (s.reshape(2, t, width) + bias[None]).reshape(rows, width)
        chunks = [s[:, c * LANES:(c + 1) * LANES] for c in range(width // LANES)]
        mc = chunks[0]
        for ch in chunks[1:]:
            mc = jnp.maximum(mc, ch)
        m_prev = m_ref[...]
        m_new = jnp.maximum(m_prev, jnp.max(mc, axis=-1, keepdims=True))
        alpha = jnp.exp2(m_prev - m_new)
        p = jnp.concatenate([jnp.exp2(ch - m_new) for ch in chunks], axis=1).astype(BF16)
        pv = jnp.dot(p, vj, preferred_element_type=F32)
        acc_ref[...] = jnp.concatenate([alpha, alpha], axis=1) * acc_ref[...] + pv
        m_ref[...] = m_new

    def far_body(j, carry):
        step(pl.multiple_of(j * t, t), t, None)
        return carry
    lax.fori_loop(0, jnp.maximum(qi - 1, 0), far_body, 0)

    step(pl.multiple_of(jnp.maximum(qi - 1, 0) * t, t), 2 * t, nb_ref[0, 0])

    lam = lam_ref[0]
    acc = acc_ref[...]
    o = acc[:t, :hw] / acc[:t, hw:] - lam * (acc[t:, :hw] / acc[t:, hw:])
    o = o * lax.rsqrt(jnp.mean(o * o, axis=-1, keepdims=True) + RMS_EPS) * sg_ref[...]
    o_ref[0] = (o * (1.0 - LAM_INIT)).astype(o_ref.dtype)


def _diff_attention(qkv, rel_table, lam, q_gain, k_gain, sub_gain):
    b, s_len, _ = qkv.shape
    t = _tile(s_len // 2, ATT_Q_BLOCK)
    hw = 2 * A_QK_DIM
    nb = _near_bias(rel_table, t)
    tile2 = lambda g: jnp.concatenate([g, g]).reshape(1, hw).astype(F32)
    kern = functools.partial(_attn_kernel, t=t, s_len=s_len, k_chunk=_tile(s_len, 512))
    return pl.pallas_call(
        kern,
        grid=(b, A_HEADS, s_len // t),
        in_specs=[
            pl.BlockSpec(memory_space=pltpu.SMEM),
            pl.BlockSpec((1, t, hw), lambda bi, h, qi: (bi, qi, h)),
            pl.BlockSpec((1, s_len, hw), lambda bi, h, qi: (bi, 0, A_HEADS + h)),
            pl.BlockSpec((1, s_len, hw), lambda bi, h, qi: (bi, 0, 2 * A_HEADS + h)),
            pl.BlockSpec((1, 1, t, 2 * t), lambda bi, h, qi: (h, jnp.minimum(qi, 1), 0, 0)),
            pl.BlockSpec((1, hw), lambda bi, h, qi: (0, 0)),
            pl.BlockSpec((1, hw), lambda bi, h, qi: (0, 0)),
            pl.BlockSpec((1, hw), lambda bi, h, qi: (0, 0)),
        ],
        out_specs=pl.BlockSpec((1, t, hw), lambda bi, h, qi: (bi, qi, h)),
        out_shape=jax.ShapeDtypeStruct((b, s_len, A_HEADS * hw), BF16),
        scratch_shapes=[
            pltpu.VMEM((s_len, hw), BF16),
            pltpu.VMEM((s_len, 2 * hw), BF16),
            pltpu.VMEM((2 * t, hw), F32),
            pltpu.VMEM((2 * t, 2 * hw), F32),
        ],
        compiler_params=_params(("arbitrary", "arbitrary", "arbitrary"), 48),
        name="diff_attention",
    )(lam.reshape(1).astype(F32), qkv, qkv, qkv, nb, tile2(q_gain), tile2(k_gain),
      sub_gain.reshape(1, hw).astype(F32))


def _lru_kernel(xl_ref, gl_ref, cw_ref, cb_ref, wa_ref, ba_ref, wx_ref, bx_ref, c_ref, o_ref,
                xbuf, a_s, u_s, h_s, hc, *, ts, width):
    i = pl.program_id(1)
    halo = SUBLANES
    bd = width // LRU_BLOCKS

    @pl.when(i == 0)
    def _():
        xbuf[0:halo, :] = jnp.zeros((halo, width), F32)
        hc[...] = jnp.zeros(hc.shape, F32)

    x = xl_ref[0]
    xbuf[halo:halo + ts, :] = x
    xr = cb_ref[...] + cw_ref[0:1, :] * xbuf[pl.ds(halo - 3, ts), :]
    for j in range(1, CONV_WIDTH):
        xr = xr + cw_ref[j:j + 1, :] * xbuf[pl.ds(halo - 3 + j, ts), :]
    xbuf[0:halo, :] = x[ts - halo:, :]

    for g in range(LRU_BLOCKS):
        sl = slice(g * bd, (g + 1) * bd)
        xg = xr[:, sl]
        xb = xg.astype(BF16)
        r = jax.nn.sigmoid(jnp.dot(xb, wa_ref[g], preferred_element_type=F32) + ba_ref[:, sl])
        gi = jax.nn.sigmoid(jnp.dot(xb, wx_ref[g], preferred_element_type=F32) + bx_ref[:, sl])
        log_a = r * c_ref[:, sl]
        a = jnp.exp(log_a)
        a_s[:, sl] = a
        u_s[:, sl] = jnp.sqrt(-jnp.tanh(log_a) * (1.0 + a * a)) * (gi * xg)

    row = lax.broadcasted_iota(jnp.int32, (SUBLANES, width), 0)

    def body(gidx, h):
        r0 = pl.multiple_of(gidx * SUBLANES, SUBLANES)
        a = a_s[pl.ds(r0, SUBLANES), :]
        u = u_s[pl.ds(r0, SUBLANES), :]
        for d in (1, 2, 4):
            keep = row >= d
            u = jnp.where(keep, a * pltpu.roll(u, d, 0) + u, u)
            a = jnp.where(keep, a * pltpu.roll(a, d, 0), a)
        hr = a * h + u
        h_s[pl.ds(r0, SUBLANES), :] = hr
        return hr[SUBLANES - 1:SUBLANES, :]

    hc[...] = lax.fori_loop(0, ts // SUBLANES, body, hc[...], unroll=2)
    o_ref[0] = (h_s[...] * jax.nn.gelu(gl_ref[0])).astype(o_ref.dtype)


def _lru_branch(lru, conv_w, conv_b, wa, ba, wx, bx, lam_param):
    b, s_len, c2 = lru.shape
    width = c2 // 2
    ts = _tile(s_len, 256)
    bd = width // LRU_BLOCKS
    c_vec = (-LRU_C * jax.nn.softplus(-lam_param.astype(F32))).reshape(1, width)
    row = lambda v: v.reshape(1, width).astype(F32)
    const2 = lambda shape: pl.BlockSpec(shape, lambda bi, i: (0,) * len(shape))
    kern = functools.partial(_lru_kernel, ts=ts, width=width)
    return pl.pallas_call(
        kern,
        grid=(b, s_len // ts),
        in_specs=[
            pl.BlockSpec((1, ts, width), lambda bi, i: (bi, i, 0)),
            pl.BlockSpec((1, ts, width), lambda bi, i: (bi, i, 1)),
            const2((CONV_WIDTH, width)), const2((1, width)),
            const2((LRU_BLOCKS, bd, bd)), const2((1, width)),
            const2((LRU_BLOCKS, bd, bd)), const2((1, width)),
            const2((1, width)),
        ],
        out_specs=pl.BlockSpec((1, ts, width), lambda bi, i: (bi, i, 0)),
        out_shape=jax.ShapeDtypeStruct((b, s_len, width), BF16),
        scratch_shapes=[
            pltpu.VMEM((ts + SUBLANES, width), F32),
            pltpu.VMEM((ts, width), F32),
            pltpu.VMEM((ts, width), F32),
            pltpu.VMEM((ts, width), F32),
            pltpu.VMEM((1, width), F32),
        ],
        compiler_params=_params(("arbitrary", "arbitrary"), 40),
        name="rg_lru",
    )(lru, lru, conv_w.astype(F32), row(conv_b), wa.astype(BF16), row(ba), wx.astype(BF16), row(bx), c_vec)


def _merge_kernel(oa_ref, ob_ref, wa_ref, wb_ref, g0_ref, g1_ref, o_ref):
    ya = jnp.dot(oa_ref[...], wa_ref[...], preferred_element_type=F32)
    yb = jnp.dot(ob_ref[...], wb_ref[...], preferred_element_type=F32)
    o_ref[...] = (g0_ref[...].astype(F32) * ya + g1_ref[...].astype(F32) * yb).astype(o_ref.dtype)


def _merge(o_a, o_b, w_a, w_b, gates):
    m, ka = o_a.shape
    kb = o_b.shape[1]
    d = w_a.shape[1]
    tm, tn = _tile(m, 512), _tile(d, 1024)
    nj = d // tn
    return pl.pallas_call(
        _merge_kernel,
        grid=(m // tm, nj),
        in_specs=[
            pl.BlockSpec((tm, ka), lambda i, j: (i, 0)),
            pl.BlockSpec((tm, kb), lambda i, j: (i, 0)),
            pl.BlockSpec((ka, tn), lambda i, j: (0, j)),
            pl.BlockSpec((kb, tn), lambda i, j: (0, j)),
            pl.BlockSpec((tm, tn), lambda i, j: (i, j)),
            pl.BlockSpec((tm, tn), lambda i, j: (i, j + nj)),
        ],
        out_specs=pl.BlockSpec((tm, tn), lambda i, j: (i, j)),
        out_shape=jax.ShapeDtypeStruct((m, d), BF16),
        compiler_params=_params(("parallel", "parallel"), 40),
        name="branch_merge",
    )(o_a, o_b, w_a, w_b, gates, gates)


def _outproj_kernel(mx_ref, wo_ref, x_ref, g_ref, wr_ref, br_ref,
                    x1_ref, xn_ref, ti_ref, tg_ref, pos_ref, cnt_ref, carry, *, tm):
    i = pl.program_id(0)

    @pl.when(i == 0)
    def _():
        carry[...] = jnp.zeros(carry.shape, F32)

    x1 = x_ref[...] + jnp.dot(mx_ref[...], wo_ref[...], preferred_element_type=F32)
    x1_ref[...] = x1
    xn = x1 * lax.rsqrt(jnp.mean(x1 * x1, axis=-1, keepdims=True) + RMS_EPS) * g_ref[...]
    xn_ref[...] = xn
    logits = jnp.dot(xn, wr_ref[...], preferred_element_type=F32, precision=lax.Precision.HIGHEST) + br_ref[...]

    lane = lax.broadcasted_iota(jnp.int32, (tm, LANES), 1)
    rest = logits
    vals, idxs = [], []
    for _ in range(TOP_K):
        mx = jnp.max(rest, axis=-1, keepdims=True)
        ix = jnp.min(jnp.where(rest == mx, lane, LANES), axis=-1, keepdims=True)
        vals.append(mx)
        idxs.append(ix)
        rest = jnp.where(lane == ix, NEG_SEL, rest)
    exps = [jnp.exp(v - vals[0]) for v in vals]
    den = exps[0]
    for e in exps[1:]:
        den = den + e
    ti = jnp.zeros((tm, LANES), jnp.int32)
    tg = jnp.zeros((tm, LANES), F32)
    sel = jnp.zeros((tm, LANES), F32)
    for k in range(TOP_K):
        ti = jnp.where(lane == k, idxs[k], ti)
        tg = jnp.where(lane == k, exps[k] / den, tg)
        sel = jnp.where(lane == idxs[k], 1.0, sel)
    ti_ref[...] = ti
    tg_ref[...] = tg

    rr = lax.broadcasted_iota(jnp.int32, (tm, tm), 0)
    cc = lax.broadcasted_iota(jnp.int32, (tm, tm), 1)
    below = (cc < rr).astype(BF16)
    pos = jnp.dot(below, sel.astype(BF16), preferred_element_type=F32) + carry[...]
    pos4 = jnp.zeros((tm, LANES), F32)
    for k in range(TOP_K):
        pk = jnp.sum(jnp.where(lane == idxs[k], pos, 0.0), axis=-1, keepdims=True)
        pos4 = jnp.where(lane == k, pk, pos4)
    pos_ref[...] = pos4.astype(jnp.int32)
    carry[...] = carry[...] + jnp.sum(sel, axis=0, keepdims=True)
    cnt_ref[...] = carry[...].astype(jnp.int32)


def _outproj_router(mixed, w_out, x, ln_ffn, w_router, b_router):
    m, d = x.shape
    e = w_router.shape[1]
    assert e <= LANES
    tm = _tile(m, 256)
    wr = jnp.zeros((d, LANES), F32).at[:, :e].set(w_router.astype(F32))
    br = jnp.full((1, LANES), NEG_BIG, F32).at[0, :e].set(b_router.astype(F32))
    row_blk = lambda w: pl.BlockSpec((tm, w), lambda i: (i, 0))
    const = lambda shape: pl.BlockSpec(shape, lambda i: (0, 0))
    return pl.pallas_call(
        functools.partial(_outproj_kernel, tm=tm),
        grid=(m // tm,),
        in_specs=[row_blk(d), const((d, d)), row_blk(d), const((1, d)), const((d, LANES)), const((1, LANES))],
        out_specs=[row_blk(d), row_blk(d), row_blk(LANES), row_blk(LANES), row_blk(LANES), const((1, LANES))],
        out_shape=[
            jax.ShapeDtypeStruct((m, d), F32),
            jax.ShapeDtypeStruct((m, d), F32),
            jax.ShapeDtypeStruct((m, LANES), jnp.int32),
            jax.ShapeDtypeStruct((m, LANES), F32),
            jax.ShapeDtypeStruct((m, LANES), jnp.int32),
            jax.ShapeDtypeStruct((1, LANES), jnp.int32),
        ],
        scratch_shapes=[pltpu.VMEM((1, LANES), F32)],
        compiler_params=_params(("arbitrary",), 48),
        name="outproj_router",
    )(mixed, w_out, x, ln_ffn.reshape(1, d).astype(F32), wr, br)


def _gather_kernel(idx_ref, nxt_ref, src_ref, *rest, rows, groups, weighted):
    if weighted:
        g_ref, base_ref, o_ref, buf, sem = rest
    else:
        o_ref, buf, sem = rest
    i = pl.program_id(0)
    n = pl.num_programs(0)
    total = rows * groups

    def row_copy(ref, r, slot):
        return pltpu.make_async_copy(src_ref.at[pl.ds(ref[0, 0, r], 1), :],
                                     buf.at[slot, pl.ds(r, 1), :], sem.at[slot])

    def issue(ref, slot):
        def body(r, carry):
            row_copy(ref, r, slot).start()
            return carry
        lax.fori_loop(0, total, body, 0, unroll=8)

    @pl.when(i == 0)
    def _():
        issue(idx_ref, 0)

    @pl.when(i + 1 < n)
    def _():
        issue(nxt_ref, (i + 1) % 2)

    slot = i % 2

    def wait_body(r, carry):
        row_copy(idx_ref, r, slot).wait()
        return carry
    lax.fori_loop(0, total, wait_body, 0, unroll=8)

    if weighted:
        lane = lax.broadcasted_iota(jnp.int32, (rows, LANES), 1)
        g = g_ref[...]
        acc = base_ref[...]
        for k in range(groups):
            gk = jnp.sum(jnp.where(lane == k, g, 0.0), axis=-1, keepdims=True)
            acc = acc + gk * buf[slot, pl.ds(k * rows, rows), :]
        o_ref[...] = acc.astype(o_ref.dtype)
    else:
        o_ref[...] = buf[slot].astype(o_ref.dtype)


def _gather_rows(src, idx, rows, out_dtype, gates=None, base=None):
    steps, _, total = idx.shape
    groups = total // rows
    d = src.shape[1]
    weighted = gates is not None
    in_specs = [
        pl.BlockSpec((1, 1, total), lambda i: (i, 0, 0), memory_space=pltpu.SMEM),
        pl.BlockSpec((1, 1, total), lambda i: (jnp.minimum(i + 1, steps - 1), 0, 0), memory_space=pltpu.SMEM),
        pl.BlockSpec(memory_space=pl.ANY),
    ]
    args = [idx, idx, src]
    if weighted:
        in_specs += [pl.BlockSpec((rows, LANES), lambda i: (i, 0)), pl.BlockSpec((rows, d), lambda i: (i, 0))]
        args += [gates, base]
    return pl.pallas_call(
        functools.partial(_gather_kernel, rows=rows, groups=groups, weighted=weighted),
        grid=(steps,),
        in_specs=in_specs,
        out_specs=pl.BlockSpec((rows, d), lambda i: (i, 0)),
        out_shape=jax.ShapeDtypeStruct((steps * rows, d), out_dtype),
        scratch_shapes=[pltpu.VMEM((2, total, d), F32), pltpu.SemaphoreType.DMA((2,))],
        compiler_params=_params(("arbitrary",), 48),
        name="combine_rows" if weighted else "gather_rows",
    )(*args)


def _zero_unused_block(nu_ref, o_ref):
    @pl.when(pl.program_id(1) >= nu_ref[0])
    def _():
        o_ref[...] = jnp.zeros(o_ref.shape, o_ref.dtype)


def _moe1_kernel(be_ref, nu_ref, first_ref, x_ref, w_ref, bg_ref, bl_ref, o_ref, wp_ref, *, tn):
    i = pl.program_id(1)
    groups = tn // MXU_DIM
    half = MXU_DIM // 2
    _zero_unused_block(nu_ref, o_ref)

    @pl.when(first_ref[i] == 1)
    def _():
        rr = lax.broadcasted_iota(jnp.int32, (MXU_DIM, MXU_DIM), 0)
        cc = lax.broadcasted_iota(jnp.int32, (MXU_DIM, MXU_DIM), 1)
        src = jnp.where(cc < half, 2 * cc, 2 * (cc - half) + 1)
        perm = (rr == src).astype(BF16)
        for g in range(groups):
            sl = slice(g * MXU_DIM, (g + 1) * MXU_DIM)
            wt = w_ref[0, :, sl].astype(BF16)
            wp_ref[:, sl] = jnp.dot(wt, perm, preferred_element_type=F32).astype(BF16)

    @pl.when(i < nu_ref[0])
    def _():
        h = jnp.dot(x_ref[...], wp_ref[...], preferred_element_type=F32)
        for g in range(groups):
            fs = slice(g * half, (g + 1) * half)
            hg = h[:, g * MXU_DIM:g * MXU_DIM + half] + bg_ref[0, :, fs]
            hl = h[:, g * MXU_DIM + half:(g + 1) * MXU_DIM] + bl_ref[0, :, fs]
            glu = jnp.minimum(hg, SWIGLU_LIMIT)
            lin = jnp.clip(hl, -SWIGLU_LIMIT, SWIGLU_LIMIT)
            o_ref[:, fs] = (glu * jax.nn.sigmoid(SWIGLU_ALPHA * glu) * (lin + 1.0)).astype(o_ref.dtype)


def _moe2_kernel(be_ref, nu_ref, first_ref, a_ref, w_ref, b_ref, o_ref, wb_ref):
    i = pl.program_id(1)
    _zero_unused_block(nu_ref, o_ref)

    @pl.when(first_ref[i] == 1)
    def _():
        wb_ref[...] = w_ref[0].astype(BF16)

    @pl.when(i < nu_ref[0])
    def _():
        o_ref[...] = jnp.dot(a_ref[...], wb_ref[...], preferred_element_type=F32) + b_ref[0]


def _grouped(kernel, name, rows_in, weight, biases, sched, bm, tn, out_cols_per_tile, out_dtype):
    p, k = rows_in.shape
    n = weight.shape[2]
    nblk = p // bm
    tb = out_cols_per_tile
    blk = lambda i, nu: jnp.minimum(i, nu[0] - 1)
    grid_spec = pltpu.PrefetchScalarGridSpec(
        num_scalar_prefetch=3,
        grid=(n // tn, nblk),
        in_specs=[pl.BlockSpec((bm, k), lambda j, i, be, nu, fi: (blk(i, nu), 0)),
                  pl.BlockSpec((1, k, tn), lambda j, i, be, nu, fi: (be[i], 0, j))]
        + [pl.BlockSpec((1, 1, tb), lambda j, i, be, nu, fi: (be[i], 0, j))] * len(biases),
        out_specs=pl.BlockSpec((bm, tb), lambda j, i, be, nu, fi: (i, j)),
        scratch_shapes=[pltpu.VMEM((k, tn), BF16)],
    )
    return pl.pallas_call(
        kernel,
        grid_spec=grid_spec,
        out_shape=jax.ShapeDtypeStruct((p, (n // tn) * tb), out_dtype),
        compiler_params=_params(("arbitrary", "arbitrary"), 56),
        name=name,
    )(*sched, rows_in, weight, *biases)


def _moe(x1, xn, ti, tg, pos, cnt, w1, b1, w2, b2):
    t_tok, d = x1.shape
    n_exp, _, f2 = w1.shape
    f = f2 // 2
    bm = MOE_ROWS
    assert (t_tok * TOP_K) % bm == 0
    nblk = t_tok * TOP_K // bm + n_exp
    p = nblk * bm

    ti4 = ti[:, :TOP_K]
    counts = cnt[0, :n_exp]
    padded = (counts + bm - 1) // bm * bm
    pad_ends = jnp.cumsum(padded)
    pad_starts = pad_ends - padded
    onehot = ti4[:, :, None] == jnp.arange(n_exp, dtype=jnp.int32)[None, None, :]
    dest4 = jnp.sum(jnp.where(onehot, pad_starts[None, None, :], 0), axis=-1) + pos[:, :TOP_K]
    n_used = (pad_ends[-1] // bm).astype(jnp.int32).reshape(1)
    blk_start = jnp.arange(nblk, dtype=jnp.int32) * bm
    blk_start = jnp.minimum(blk_start, pad_ends[-1] - bm)
    blk_expert = jnp.sum(blk_start[:, None] >= pad_ends[None, :], axis=1).astype(jnp.int32)
    blk_expert = jnp.minimum(blk_expert, n_exp - 1)
    first = jnp.concatenate([jnp.ones((1,), jnp.int32), (blk_expert[1:] != blk_expert[:-1]).astype(jnp.int32)])
    sched = (blk_expert, n_used, first)
    tok = jnp.broadcast_to(jnp.arange(t_tok, dtype=jnp.int32)[:, None], (t_tok, TOP_K))
    row_tok = jnp.zeros((p,), jnp.int32).at[dest4.reshape(-1)].set(tok.reshape(-1), unique_indices=True)

    gr = _tile(p, GATHER_ROWS)
    xg = _gather_rows(xn, row_tok.reshape(p // gr, 1, gr), gr, BF16)

    b1g = b1[:, 0::2].reshape(n_exp, 1, f).astype(F32)
    b1l = b1[:, 1::2].reshape(n_exp, 1, f).astype(F32)
    tn1 = _tile(f2, 1024)
    assert tn1 % MXU_DIM == 0
    act = _grouped(functools.partial(_moe1_kernel, tn=tn1), "expert_up", xg, w1, [b1g, b1l], sched,
                   bm, tn1, tn1 // 2, BF16)
    tn2 = _tile(d, 1024)
    out = _grouped(_moe2_kernel, "expert_down", act, w2, [b2.reshape(n_exp, 1, d).astype(F32)], sched,
                   bm, tn2, tn2, F32)

    tc = _tile(t_tok, COMBINE_TOKENS)
    steps = t_tok // tc
    idx = jnp.transpose(dest4.reshape(steps, tc, TOP_K), (0, 2, 1)).reshape(steps, 1, TOP_K * tc)
    return _gather_rows(out, idx, tc, F32, gates=tg, base=x1)


def kernel(x, ln_mix, w_in, b_gate, q_gain, k_gain, lambda_q1, lambda_k1, lambda_q2, lambda_k2, sub_gain, rel_table, conv_w, conv_b, lru_wa, lru_ba, lru_wx, lru_bx, lru_lambda, w_branch_a, w_branch_b, w_out, ln_ffn, w_router, b_router, w1, b1, w2, b2):
    b, s_len, d = x.shape
    t_tok = b * s_len
    depth = ln_mix.shape[0]
    assert depth == 1
    q_cols = A_HEADS * 2 * A_QK_DIM
    qkv_cols = 2 * q_cols + A_HEADS * A_V_DIM
    lru_w = d // 2
    xt = x.reshape(t_tok, d)
    l = 0

    hn = _rmsnorm(xt, ln_mix[l], BF16)
    w_in_b = w_in[l].astype(BF16)
    qkv = _proj(hn, w_in_b[:, :qkv_cols], BF16)
    lru = _proj(hn, w_in_b[:, qkv_cols:qkv_cols + 2 * lru_w], F32)
    gates = _proj(hn, w_in_b[:, qkv_cols + 2 * lru_w:], BF16, bias=b_gate[l])

    lam = (jnp.exp(jnp.sum(lambda_q1[l].astype(F32) * lambda_k1[l].astype(F32)))
           - jnp.exp(jnp.sum(lambda_q2[l].astype(F32) * lambda_k2[l].astype(F32))) + LAM_INIT)
    o_a = _diff_attention(qkv.reshape(b, s_len, qkv_cols), rel_table, lam, q_gain[l], k_gain[l], sub_gain[l])
    o_b = _lru_branch(lru.reshape(b, s_len, 2 * lru_w), conv_w[l], conv_b[l], lru_wa[l], lru_ba[l],
                      lru_wx[l], lru_bx[l], lru_lambda[l])

    mixed = _merge(o_a.reshape(t_tok, -1), o_b.reshape(t_tok, lru_w),
                   w_branch_a[l].astype(BF16), w_branch_b[l].astype(BF16), gates)
    x1, xn, ti, tg, pos, cnt = _outproj_router(mixed, w_out[l].astype(BF16), xt, ln_ffn[l], w_router[l], b_router[l])
    y = _moe(x1, xn, ti, tg, pos, cnt, w1[l], b1[l], w2[l], b2[l])
    return y.reshape(b, s_len, d)
```

```python
import functools
import math

import jax
import jax.numpy as jnp
from jax import lax
from jax.experimental import pallas as pl
from jax.experimental.pallas import tpu as pltpu

F32 = jnp.float32
BF16 = jnp.bfloat16

CHUNK = 64
RMS_EPS = 1e-6
A_HEADS = 8
A_QK_DIM = 64
A_V_DIM = 2 * A_QK_DIM
LRU_BLOCKS = 8
CONV_WIDTH = 4
LRU_C = 8.0
REL_BUCKETS = 32
REL_MAX_DIST = 128
TOP_K = 4
SWIGLU_LIMIT = 7.0
SWIGLU_ALPHA = 1.702
LAM_INIT = 0.8 - 0.6 * math.exp(-0.3 * 0)
LOG2E = 1.4426950408889634

LANES = 128
SUBLANES = 8
MXU_DIM = 256
NEG_BIG = -1e30
NEG_SEL = -3e38

ATT_Q_BLOCK = 512
MOE_ROWS = 512
GATHER_ROWS = 512
COMBINE_TOKENS = 128


def _tile(n, pref):
    t = min(n, pref)
    assert n % t == 0, (n, pref)
    return t


def _params(semantics, vmem_mib):
    return pltpu.CompilerParams(dimension_semantics=semantics, vmem_limit_bytes=vmem_mib * 1024 * 1024)


def _rmsnorm_kernel(x_ref, g_ref, o_ref):
    x = x_ref[...]
    y = x * lax.rsqrt(jnp.mean(x * x, axis=-1, keepdims=True) + RMS_EPS)
    o_ref[...] = (y * g_ref[...]).astype(o_ref.dtype)


def _rmsnorm(x, g, out_dtype):
    m, d = x.shape
    tm = _tile(m, 512)
    return pl.pallas_call(
        _rmsnorm_kernel,
        grid=(m // tm,),
        in_specs=[pl.BlockSpec((tm, d), lambda i: (i, 0)), pl.BlockSpec((1, d), lambda i: (0, 0))],
        out_specs=pl.BlockSpec((tm, d), lambda i: (i, 0)),
        out_shape=jax.ShapeDtypeStruct((m, d), out_dtype),
        compiler_params=_params(("parallel",), 32),
        name="rmsnorm",
    )(x, g.reshape(1, d))


def _proj_kernel(a_ref, w_ref, *rest, sigmoid_bias):
    if sigmoid_bias:
        b_ref, o_ref = rest
    else:
        (o_ref,) = rest
    acc = jnp.dot(a_ref[...], w_ref[...], preferred_element_type=F32)
    if sigmoid_bias:
        acc = jax.nn.sigmoid(acc + b_ref[...])
    o_ref[...] = acc.astype(o_ref.dtype)


def _proj(a, w, out_dtype, bias=None, tm=1024, tn=1024):
    m, k = a.shape
    n = w.shape[1]
    tm, tn = _tile(m, tm), _tile(n, tn)
    in_specs = [pl.BlockSpec((tm, k), lambda i, j: (i, 0)), pl.BlockSpec((k, tn), lambda i, j: (0, j))]
    args = [a, w]
    if bias is not None:
        in_specs.append(pl.BlockSpec((1, tn), lambda i, j: (0, j)))
        args.append(bias.reshape(1, n))
    return pl.pallas_call(
        functools.partial(_proj_kernel, sigmoid_bias=bias is not None),
        grid=(m // tm, n // tn),
        in_specs=in_specs,
        out_specs=pl.BlockSpec((tm, tn), lambda i, j: (i, j)),
        out_shape=jax.ShapeDtypeStruct((m, n), out_dtype),
        compiler_params=_params(("parallel", "parallel"), 48),
        name="proj",
    )(*args)


def _t5_bucket(rel):
    nb = REL_BUCKETS // 2
    max_exact = nb // 2
    ret = jnp.where(rel > 0, nb, 0)
    n = jnp.abs(rel)
    nf = jnp.maximum(n, 1).astype(F32)
    large = max_exact + (jnp.log(nf / max_exact) / math.log(REL_MAX_DIST / max_exact)
                         * (nb - max_exact)).astype(jnp.int32)
    large = jnp.minimum(large, nb - 1)
    return ret + jnp.where(n < max_exact, n, large)


def _near_bias(rel_table, t):
    assert t + 1 >= REL_MAX_DIST
    table = rel_table.astype(F32)
    r = jnp.arange(t, dtype=jnp.int32)[:, None]
    c = jnp.arange(t, dtype=jnp.int32)[None, :]

    def lookup(bucket):
        out = jnp.zeros((A_HEADS,) + bucket.shape, F32)
        for bkt in range(REL_BUCKETS):
            out = jnp.where(bucket[None] == bkt, table[bkt][:, None, None], out)
        return out

    far = table[_t5_bucket(jnp.int32(-(t + 1)))][:, None, None]
    diag = lookup(_t5_bucket(c - r)) - far
    allowed = (c // CHUNK) <= (r // CHUNK)
    diag = jnp.where(allowed[None], diag * LOG2E, NEG_BIG)
    sub = (lookup(_t5_bucket(c - r - t)) - far) * LOG2E
    future = jnp.full_like(diag, NEG_BIG)
    first = jnp.concatenate([diag, future], axis=-1)
    later = jnp.concatenate([sub, diag], axis=-1)
    return jnp.stack([first, later], axis=1)


def _attn_kernel(lam_ref, q_ref, k_ref, v_ref, nb_ref, qg_ref, kg_ref, sg_ref, o_ref,
                 kn_ref, vx_ref, m_ref, acc_ref, *, t, s_len, k_chunk):
    qi = pl.program_id(2)
    hw = 2 * A_QK_DIM
    rows = 2 * t
    lo = lax.broadcasted_iota(jnp.int32, (1, hw), 1) < A_QK_DIM

    def qk_norm(x, g):
        sq = x * x
        s_lo = jnp.sum(jnp.where(lo, sq, 0.0), axis=-1, keepdims=True)
        s_hi = jnp.sum(jnp.where(lo, 0.0, sq), axis=-1, keepdims=True)
        ms = jnp.where(lo, s_lo, s_hi) * (1.0 / A_QK_DIM)
        return x * lax.rsqrt(ms + RMS_EPS) * g

    @pl.when(qi == 0)
    def _():
        def body(c, carry):
            r0 = pl.multiple_of(c * k_chunk, k_chunk)
            kk = k_ref[0, pl.ds(r0, k_chunk), :].astype(F32)
            kn_ref[pl.ds(r0, k_chunk), :] = qk_norm(kk, kg_ref[...]).astype(BF16)
            vx_ref[pl.ds(r0, k_chunk), 0:hw] = v_ref[0, pl.ds(r0, k_chunk), :]
            vx_ref[pl.ds(r0, k_chunk), hw:2 * hw] = jnp.ones((k_chunk, hw), BF16)
            return carry
        lax.fori_loop(0, s_len // k_chunk, body, 0)

    q = qk_norm(q_ref[0].astype(F32), qg_ref[...]) * (A_QK_DIM ** -0.5 * LOG2E)
    qs = jnp.concatenate([jnp.where(lo, q, 0.0), jnp.where(lo, 0.0, q)], axis=0).astype(BF16)

    m_ref[...] = jnp.full(m_ref.shape, NEG_BIG, F32)
    acc_ref[...] = jnp.zeros(acc_ref.shape, F32)

    def step(r0, width, bias):
        kj = kn_ref[pl.ds(r0, width), :]
        vj = vx_ref[pl.ds(r0, width), :]
        s = lax.dot_general(qs, kj, (((1,), (1,)), ((), ())), preferred_element_type=F32)
        if bias is not None:
            s = (s.reshape(2, t, width) + bias[None]).reshape(rows, width)
        chunks = [s[:, c * LANES:(c + 1) * LANES] for c in range(width // LANES)]
        mc = chunks[0]
        for ch in chunks[1:]:
            mc = jnp.maximum(mc, ch)
        m_prev = m_ref[...]
        m_new = jnp.maximum(m_prev, jnp.max(mc, axis=-1, keepdims=True))
        alpha = jnp.exp2(m_prev - m_new)
        p = jnp.concatenate([jnp.exp2(ch - m_new) for ch in chunks], axis=1).astype(BF16)
        pv = jnp.dot(p, vj, preferred_element_type=F32)
        acc_ref[...] = jnp.concatenate([alpha, alpha], axis=1) * acc_ref[...] + pv
        m_ref[...] = m_new

    def far_body(j, carry):
        step(pl.multiple_of(j * t, t), t, None)
        return carry
    lax.fori_loop(0, jnp.maximum(qi - 1, 0), far_body, 0)

    step(pl.multiple_of(jnp.maximum(qi - 1, 0) * t, t), 2 * t, nb_ref[0, 0])

    lam = lam_ref[0]
    acc = acc_ref[...]
    o = acc[:t, :hw] / acc[:t, hw:] - lam * (acc[t:, :hw] / acc[t:, hw:])
    o = o * lax.rsqrt(jnp.mean(o * o, axis=-1, keepdims=True) + RMS_EPS) * sg_ref[...]
    o_ref[0] = (o * (1.0 - LAM_INIT)).astype(o_ref.dtype)


def _diff_attention(qkv, rel_table, lam, q_gain, k_gain, sub_gain):
    b, s_len, _ = qkv.shape
    t = _tile(s_len // 2, ATT_Q_BLOCK)
    hw = 2 * A_QK_DIM
    nb = _near_bias(rel_table, t)
    tile2 = lambda g: jnp.concatenate([g, g]).reshape(1, hw).astype(F32)
    kern = functools.partial(_attn_kernel, t=t, s_len=s_len, k_chunk=_tile(s_len, 512))
    return pl.pallas_call(
        kern,
        grid=(b, A_HEADS, s_len // t),
        in_specs=[
            pl.BlockSpec(memory_space=pltpu.SMEM),
            pl.BlockSpec((1, t, hw), lambda bi, h, qi: (bi, qi, h)),
            pl.BlockSpec((1, s_len, hw), lambda bi, h, qi: (bi, 0, A_HEADS + h)),
            pl.BlockSpec((1, s_len, hw), lambda bi, h, qi: (bi, 0, 2 * A_HEADS + h)),
            pl.BlockSpec((1, 1, t, 2 * t), lambda bi, h, qi: (h, jnp.minimum(qi, 1), 0, 0)),
            pl.BlockSpec((1, hw), lambda bi, h, qi: (0, 0)),
            pl.BlockSpec((1, hw), lambda bi, h, qi: (0, 0)),
            pl.BlockSpec((1, hw), lambda bi, h, qi: (0, 0)),
        ],
        out_specs=pl.BlockSpec((1, t, hw), lambda bi, h, qi: (bi, qi, h)),
        out_shape=jax.ShapeDtypeStruct((b, s_len, A_HEADS * hw), BF16),
        scratch_shapes=[
            pltpu.VMEM((s_len, hw), BF16),
            pltpu.VMEM((s_len, 2 * hw), BF16),
            pltpu.VMEM((2 * t, hw), F32),
            pltpu.VMEM((2 * t, 2 * hw), F32),
        ],
        compiler_params=_params(("arbitrary", "arbitrary", "arbitrary"), 48),
        name="diff_attention",
    )(lam.reshape(1).astype(F32), qkv, qkv, qkv, nb, tile2(q_gain), tile2(k_gain),
      sub_gain.reshape(1, hw).astype(F32))


def _lru_kernel(xl_ref, gl_ref, cw_ref, cb_ref, wa_ref, ba_ref, wx_ref, bx_ref, c_ref, o_ref,
                xbuf, a_s, u_s, h_s, hc, *, ts, width):
    i = pl.program_id(1)
    halo = SUBLANES
    bd = width // LRU_BLOCKS

    @pl.when(i == 0)
    def _():
        xbuf[0:halo, :] = jnp.zeros((halo, width), F32)
        hc[...] = jnp.zeros(hc.shape, F32)

    x = xl_ref[0]
    xbuf[halo:halo + ts, :] = x
    xr = cb_ref[...] + cw_ref[0:1, :] * xbuf[pl.ds(halo - 3, ts), :]
    for j in range(1, CONV_WIDTH):
        xr = xr + cw_ref[j:j + 1, :] * xbuf[pl.ds(halo - 3 + j, ts), :]
    xbuf[0:halo, :] = x[ts - halo:, :]

    for g in range(LRU_BLOCKS):
        sl = slice(g * bd, (g + 1) * bd)
        xg = xr[:, sl]
        xb = xg.astype(BF16)
        r = jax.nn.sigmoid(jnp.dot(xb, wa_ref[g], preferred_element_type=F32) + ba_ref[:, sl])
        gi = jax.nn.sigmoid(jnp.dot(xb, wx_ref[g], preferred_element_type=F32) + bx_ref[:, sl])
        log_a = r * c_ref[:, sl]
        a = jnp.exp(log_a)
        a_s[:, sl] = a
        u_s[:, sl] = jnp.sqrt(-jnp.tanh(log_a) * (1.0 + a * a)) * (gi * xg)

    row = lax.broadcasted_iota(jnp.int32, (SUBLANES, width), 0)

    def body(gidx, h):
        r0 = pl.multiple_of(gidx * SUBLANES, SUBLANES)
        a = a_s[pl.ds(r0, SUBLANES), :]
        u = u_s[pl.ds(r0, SUBLANES), :]
        for d in (1, 2, 4):
            keep = row >= d
            u = jnp.where(keep, a * pltpu.roll(u, d, 0) + u, u)
            a = jnp.where(keep, a * pltpu.roll(a, d, 0), a)
        hr = a * h + u
        h_s[pl.ds(r0, SUBLANES), :] = hr
        return hr[SUBLANES - 1:SUBLANES, :]

    hc[...] = lax.fori_loop(0, ts // SUBLANES, body, hc[...], unroll=2)
    o_ref[0] = (h_s[...] * jax.nn.gelu(gl_ref[0])).astype(o_ref.dtype)


def _lru_branch(lru, conv_w, conv_b, wa, ba, wx, bx, lam_param):
    b, s_len, c2 = lru.shape
    width = c2 // 2
    ts = _tile(s_len, 256)
    bd = width // LRU_BLOCKS
    c_vec = (-LRU_C * jax.nn.softplus(-lam_param.astype(F32))).reshape(1, width)
    row = lambda v: v.reshape(1, width).astype(F32)
    const2 = lambda shape: pl.BlockSpec(shape, lambda bi, i: (0,) * len(shape))
    kern = functools.partial(_lru_kernel, ts=ts, width=width)
    return pl.pallas_call(
        kern,
        grid=(b, s_len // ts),
        in_specs=[
            pl.BlockSpec((1, ts, width), lambda bi, i: (bi, i, 0)),
            pl.BlockSpec((1, ts, width), lambda bi, i: (bi, i, 1)),
            const2((CONV_WIDTH, width)), const2((1, width)),
            const2((LRU_BLOCKS, bd, bd)), const2((1, width)),
            const2((LRU_BLOCKS, bd, bd)), const2((1, width)),
            const2((1, width)),
        ],
        out_specs=pl.BlockSpec((1, ts, width), lambda bi, i: (bi, i, 0)),
        out_shape=jax.ShapeDtypeStruct((b, s_len, width), BF16),
        scratch_shapes=[
            pltpu.VMEM((ts + SUBLANES, width), F32),
            pltpu.VMEM((ts, width), F32),
            pltpu.VMEM((ts, width), F32),
            pltpu.VMEM((ts, width), F32),
            pltpu.VMEM((1, width), F32),
        ],
        compiler_params=_params(("arbitrary", "arbitrary"), 40),
        name="rg_lru",
    )(lru, lru, conv_w.astype(F32), row(conv_b), wa.astype(BF16), row(ba), wx.astype(BF16), row(bx), c_vec)


def _merge_kernel(oa_ref, ob_ref, wa_ref, wb_ref, g0_ref, g1_ref, o_ref):
    ya = jnp.dot(oa_ref[...], wa_ref[...], preferred_element_type=F32)
    yb = jnp.dot(ob_ref[...], wb_ref[...], preferred_element_type=F32)
    o_ref[...] = (g0_ref[...].astype(F32) * ya + g1_ref[...].astype(F32) * yb).astype(o_ref.dtype)


def _merge(o_a, o_b, w_a, w_b, gates):
    m, ka = o_a.shape
    kb = o_b.shape[1]
    d = w_a.shape[1]
    tm, tn = _tile(m, 512), _tile(d, 1024)
    nj = d // tn
    return pl.pallas_call(
        _merge_kernel,
        grid=(m // tm, nj),
        in_specs=[
            pl.BlockSpec((tm, ka), lambda i, j: (i, 0)),
            pl.BlockSpec((tm, kb), lambda i, j: (i, 0)),
            pl.BlockSpec((ka, tn), lambda i, j: (0, j)),
            pl.BlockSpec((kb, tn), lambda i, j: (0, j)),
            pl.BlockSpec((tm, tn), lambda i, j: (i, j)),
            pl.BlockSpec((tm, tn), lambda i, j: (i, j + nj)),
        ],
        out_specs=pl.BlockSpec((tm, tn), lambda i, j: (i, j)),
        out_shape=jax.ShapeDtypeStruct((m, d), BF16),
        compiler_params=_params(("parallel", "parallel"), 40),
        name="branch_merge",
    )(o_a, o_b, w_a, w_b, gates, gates)


def _outproj_kernel(mx_ref, wo_ref, x_ref, g_ref, wr_ref, br_ref,
                    x1_ref, xn_ref, ti_ref, tg_ref, pos_ref, cnt_ref, carry, *, tm):
    i = pl.program_id(0)

    @pl.when(i == 0)
    def _():
        carry[...] = jnp.zeros(carry.shape, F32)

    x1 = x_ref[...] + jnp.dot(mx_ref[...], wo_ref[...], preferred_element_type=F32)
    x1_ref[...] = x1
    xn = x1 * lax.rsqrt(jnp.mean(x1 * x1, axis=-1, keepdims=True) + RMS_EPS) * g_ref[...]
    xn_ref[...] = xn
    logits = jnp.dot(xn, wr_ref[...], preferred_element_type=F32, precision=lax.Precision.HIGHEST) + br_ref[...]

    lane = lax.broadcasted_iota(jnp.int32, (tm, LANES), 1)
    rest = logits
    vals, idxs = [], []
    for _ in range(TOP_K):
        mx = jnp.max(rest, axis=-1, keepdims=True)
        ix = jnp.min(jnp.where(rest == mx, lane, LANES), axis=-1, keepdims=True)
        vals.append(mx)
        idxs.append(ix)
        rest = jnp.where(lane == ix, NEG_SEL, rest)
    exps = [jnp.exp(v - vals[0]) for v in vals]
    den = exps[0]
    for e in exps[1:]:
        den = den + e
    ti = jnp.zeros((tm, LANES), jnp.int32)
    tg = jnp.zeros((tm, LANES), F32)
    sel = jnp.zeros((tm, LANES), F32)
    for k in range(TOP_K):
        ti = jnp.where(lane == k, idxs[k], ti)
        tg = jnp.where(lane == k, exps[k] / den, tg)
        sel = jnp.where(lane == idxs[k], 1.0, sel)
    ti_ref[...] = ti
    tg_ref[...] = tg

    rr = lax.broadcasted_iota(jnp.int32, (tm, tm), 0)
    cc = lax.broadcasted_iota(jnp.int32, (tm, tm), 1)
    below = (cc < rr).astype(BF16)
    pos = jnp.dot(below, sel.astype(BF16), preferred_element_type=F32) + carry[...]
    pos4 = jnp.zeros((tm, LANES), F32)
    for k in range(TOP_K):
        pk = jnp.sum(jnp.where(lane == idxs[k], pos, 0.0), axis=-1, keepdims=True)
        pos4 = jnp.where(lane == k, pk, pos4)
    pos_ref[...] = pos4.astype(jnp.int32)
    carry[...] = carry[...] + jnp.sum(sel, axis=0, keepdims=True)
    cnt_ref[...] = carry[...].astype(jnp.int32)


def _outproj_router(mixed, w_out, x, ln_ffn, w_router, b_router):
    m, d = x.shape
    e = w_router.shape[1]
    assert e <= LANES
    tm = _tile(m, 512)
    wr = jnp.zeros((d, LANES), F32).at[:, :e].set(w_router.astype(F32))
    br = jnp.full((1, LANES), NEG_BIG, F32).at[0, :e].set(b_router.astype(F32))
    row_blk = lambda w: pl.BlockSpec((tm, w), lambda i: (i, 0))
    const = lambda shape: pl.BlockSpec(shape, lambda i: (0, 0))
    return pl.pallas_call(
        functools.partial(_outproj_kernel, tm=tm),
        grid=(m // tm,),
        in_specs=[row_blk(d), const((d, d)), row_blk(d), const((1, d)), const((d, LANES)), const((1, LANES))],
        out_specs=[row_blk(d), row_blk(d), row_blk(LANES), row_blk(LANES), row_blk(LANES), const((1, LANES))],
        out_shape=[
            jax.ShapeDtypeStruct((m, d), F32),
            jax.ShapeDtypeStruct((m, d), F32),
            jax.ShapeDtypeStruct((m, LANES), jnp.int32),
            jax.ShapeDtypeStruct((m, LANES), F32),
            jax.ShapeDtypeStruct((m, LANES), jnp.int32),
            jax.ShapeDtypeStruct((1, LANES), jnp.int32),
        ],
        scratch_shapes=[pltpu.VMEM((1, LANES), F32)],
        compiler_params=_params(("arbitrary",), 60),
        name="outproj_router",
    )(mixed, w_out, x, ln_ffn.reshape(1, d).astype(F32), wr, br)


def _gather_kernel(nv_ref, idx_ref, nxt_ref, src_ref, *rest, rows, groups, weighted):
    if weighted:
        g_ref, base_ref, o_ref, buf, sem = rest
    else:
        o_ref, buf, sem = rest
    i = pl.program_id(0)
    n = nv_ref[0]
    total = rows * groups

    def row_copy(ref, r, slot):
        return pltpu.make_async_copy(src_ref.at[pl.ds(ref[0, 0, r], 1), :],
                                     buf.at[slot, pl.ds(r, 1), :], sem.at[slot])

    def issue(ref, slot):
        def body(r, carry):
            row_copy(ref, r, slot).start()
            return carry
        lax.fori_loop(0, total, body, 0, unroll=8)

    @pl.when(jnp.logical_and(i == 0, n > 0))
    def _():
        issue(idx_ref, 0)

    @pl.when(i + 1 < n)
    def _():
        issue(nxt_ref, (i + 1) % 2)

    slot = i % 2

    @pl.when(i < n)
    def _():
        def wait_body(r, carry):
            row_copy(idx_ref, r, slot).wait()
            return carry
        lax.fori_loop(0, total, wait_body, 0, unroll=8)

        if weighted:
            lane = lax.broadcasted_iota(jnp.int32, (rows, LANES), 1)
            g = g_ref[...]
            acc = base_ref[...]
            for k in range(groups):
                gk = jnp.sum(jnp.where(lane == k, g, 0.0), axis=-1, keepdims=True)
                acc = acc + gk * buf[slot, pl.ds(k * rows, rows), :]
            o_ref[...] = acc.astype(o_ref.dtype)
        else:
            o_ref[...] = buf[slot].astype(o_ref.dtype)

    @pl.when(i >= n)
    def _():
        o_ref[...] = jnp.zeros(o_ref.shape, o_ref.dtype)


def _gather_rows(src, idx, n_valid, rows, out_dtype, gates=None, base=None):
    steps, _, total = idx.shape
    groups = total // rows
    d = src.shape[1]
    weighted = gates is not None
    in_specs = [
        pl.BlockSpec(memory_space=pltpu.SMEM),
        pl.BlockSpec((1, 1, total), lambda i: (i, 0, 0), memory_space=pltpu.SMEM),
        pl.BlockSpec((1, 1, total), lambda i: (jnp.minimum(i + 1, steps - 1), 0, 0), memory_space=pltpu.SMEM),
        pl.BlockSpec(memory_space=pl.ANY),
    ]
    args = [n_valid, idx, idx, src]
    if weighted:
        in_specs += [pl.BlockSpec((rows, LANES), lambda i: (i, 0)), pl.BlockSpec((rows, d), lambda i: (i, 0))]
        args += [gates, base]
    return pl.pallas_call(
        functools.partial(_gather_kernel, rows=rows, groups=groups, weighted=weighted),
        grid=(steps,),
        in_specs=in_specs,
        out_specs=pl.BlockSpec((rows, d), lambda i: (i, 0)),
        out_shape=jax.ShapeDtypeStruct((steps * rows, d), out_dtype),
        scratch_shapes=[pltpu.VMEM((2, total, d), F32), pltpu.SemaphoreType.DMA((2,))],
        compiler_params=_params(("arbitrary",), 48),
        name="combine_rows" if weighted else "gather_rows",
    )(*args)


def _zero_unused_block(nu_ref, o_ref):
    @pl.when(pl.program_id(1) >= nu_ref[0])
    def _():
        o_ref[...] = jnp.zeros(o_ref.shape, o_ref.dtype)


def _moe1_kernel(be_ref, nu_ref, first_ref, x_ref, w_ref, bg_ref, bl_ref, o_ref, wp_ref, *, tn):
    i = pl.program_id(1)
    groups = tn // MXU_DIM
    half = MXU_DIM // 2
    _zero_unused_block(nu_ref, o_ref)

    @pl.when(first_ref[i] == 1)
    def _():
        rr = lax.broadcasted_iota(jnp.int32, (MXU_DIM, MXU_DIM), 0)
        cc = lax.broadcasted_iota(jnp.int32, (MXU_DIM, MXU_DIM), 1)
        src = jnp.where(cc < half, 2 * cc, 2 * (cc - half) + 1)
        perm = (rr == src).astype(BF16)
        for g in range(groups):
            sl = slice(g * MXU_DIM, (g + 1) * MXU_DIM)
            wt = w_ref[0, :, sl].astype(BF16)
            wp_ref[:, sl] = jnp.dot(wt, perm, preferred_element_type=F32).astype(BF16)

    @pl.when(i < nu_ref[0])
    def _():
        h = jnp.dot(x_ref[...], wp_ref[...], preferred_element_type=F32)
        for g in range(groups):
            fs = slice(g * half, (g + 1) * half)
            hg = h[:, g * MXU_DIM:g * MXU_DIM + half] + bg_ref[0, :, fs]
            hl = h[:, g * MXU_DIM + half:(g + 1) * MXU_DIM] + bl_ref[0, :, fs]
            glu = jnp.minimum(hg, SWIGLU_LIMIT)
            lin = jnp.clip(hl, -SWIGLU_LIMIT, SWIGLU_LIMIT)
            o_ref[:, fs] = (glu * jax.nn.sigmoid(SWIGLU_ALPHA * glu) * (lin + 1.0)).astype(o_ref.dtype)


def _moe2_kernel(be_ref, nu_ref, first_ref, a_ref, w_ref, b_ref, o_ref, wb_ref):
    i = pl.program_id(1)
    _zero_unused_block(nu_ref, o_ref)

    @pl.when(first_ref[i] == 1)
    def _():
        wb_ref[...] = w_ref[0].astype(BF16)

    @pl.when(i < nu_ref[0])
    def _():
        o_ref[...] = jnp.dot(a_ref[...], wb_ref[...], preferred_element_type=F32) + b_ref[0]


def _grouped(kernel, name, rows_in, weight, biases, sched, bm, tn, out_cols_per_tile, out_dtype):
    p, k = rows_in.shape
    n = weight.shape[2]
    nblk = p // bm
    tb = out_cols_per_tile
    blk = lambda i, nu: jnp.maximum(jnp.minimum(i, nu[0] - 1), 0)
    grid_spec = pltpu.PrefetchScalarGridSpec(
        num_scalar_prefetch=3,
        grid=(n // tn, nblk),
        in_specs=[pl.BlockSpec((bm, k), lambda j, i, be, nu, fi: (blk(i, nu), 0)),
                  pl.BlockSpec((1, k, tn), lambda j, i, be, nu, fi: (be[i], 0, j))]
        + [pl.BlockSpec((1, 1, tb), lambda j, i, be, nu, fi: (be[i], 0, j))] * len(biases),
        out_specs=pl.BlockSpec((bm, tb), lambda j, i, be, nu, fi: (i, j)),
        scratch_shapes=[pltpu.VMEM((k, tn), BF16)],
    )
    return pl.pallas_call(
        kernel,
        grid_spec=grid_spec,
        out_shape=jax.ShapeDtypeStruct((p, (n // tn) * tb), out_dtype),
        compiler_params=_params(("arbitrary", "arbitrary"), 60),
        name=name,
    )(*sched, rows_in, weight, *biases)


def _moe(x1, xn, ti, tg, pos, cnt, w1, b1, w2, b2):
    t_tok, d = x1.shape
    n_exp, _, f2 = w1.shape
    f = f2 // 2
    bm = MOE_ROWS
    assert (t_tok * TOP_K) % bm == 0
    nblk = t_tok * TOP_K // bm + n_exp
    p = nblk * bm

    ti4 = ti[:, :TOP_K]
    counts = cnt[0, :n_exp]
    padded = (counts + bm - 1) // bm * bm
    pad_ends = jnp.cumsum(padded)
    pad_starts = pad_ends - padded
    onehot = ti4[:, :, None] == jnp.arange(n_exp, dtype=jnp.int32)[None, None, :]
    dest4 = jnp.sum(jnp.where(onehot, pad_starts[None, None, :], 0), axis=-1) + pos[:, :TOP_K]
    n_used = (pad_ends[-1] // bm).astype(jnp.int32).reshape(1)
    blk_start = jnp.arange(nblk, dtype=jnp.int32) * bm
    blk_start = jnp.minimum(blk_start, pad_ends[-1] - bm)
    blk_expert = jnp.sum(blk_start[:, None] >= pad_ends[None, :], axis=1).astype(jnp.int32)
    blk_expert = jnp.minimum(blk_expert, n_exp - 1)
    first = jnp.concatenate([jnp.ones((1,), jnp.int32), (blk_expert[1:] != blk_expert[:-1]).astype(jnp.int32)])
    sched = (blk_expert, n_used, first)
    tok = jnp.broadcast_to(jnp.arange(t_tok, dtype=jnp.int32)[:, None], (t_tok, TOP_K))
    row_tok = (jnp.arange(p, dtype=jnp.int32) % t_tok).at[dest4.reshape(-1)].set(tok.reshape(-1), unique_indices=True)

    gr = _tile(bm, GATHER_ROWS)
    assert bm % gr == 0
    xg = _gather_rows(xn, row_tok.reshape(p // gr, 1, gr), n_used * (bm // gr), gr, BF16)

    b1g = b1[:, 0::2].reshape(n_exp, 1, f).astype(F32)
    b1l = b1[:, 1::2].reshape(n_exp, 1, f).astype(F32)
    tn1 = _tile(f2, 2048)
    assert tn1 % MXU_DIM == 0
    act = _grouped(functools.partial(_moe1_kernel, tn=tn1), "expert_up", xg, w1, [b1g, b1l], sched,
                   bm, tn1, tn1 // 2, BF16)
    tn2 = _tile(d, 2048)
    out = _grouped(_moe2_kernel, "expert_down", act, w2, [b2.reshape(n_exp, 1, d).astype(F32)], sched,
                   bm, tn2, tn2, F32)

    tc = _tile(t_tok, COMBINE_TOKENS)
    steps = t_tok // tc
    idx = jnp.transpose(dest4.reshape(steps, tc, TOP_K), (0, 2, 1)).reshape(steps, 1, TOP_K * tc)
    return _gather_rows(out, idx, jnp.full((1,), steps, jnp.int32), tc, F32, gates=tg, base=x1)


def kernel(x, ln_mix, w_in, b_gate, q_gain, k_gain, lambda_q1, lambda_k1, lambda_q2, lambda_k2, sub_gain, rel_table, conv_w, conv_b, lru_wa, lru_ba, lru_wx, lru_bx, lru_lambda, w_branch_a, w_branch_b, w_out, ln_ffn, w_router, b_router, w1, b1, w2, b2):
    b, s_len, d = x.shape
    t_tok = b * s_len
    depth = ln_mix.shape[0]
    assert depth == 1
    q_cols = A_HEADS * 2 * A_QK_DIM
    qkv_cols = 2 * q_cols + A_HEADS * A_V_DIM
    lru_w = d // 2
    xt = x.reshape(t_tok, d)
    l = 0

    hn = _rmsnorm(xt, ln_mix[l], BF16)
    w_in_b = w_in[l].astype(BF16)
    qkv = _proj(hn, w_in_b[:, :qkv_cols], BF16)
    lru = _proj(hn, w_in_b[:, qkv_cols:qkv_cols + 2 * lru_w], F32)
    gates = _proj(hn, w_in_b[:, qkv_cols + 2 * lru_w:], BF16, bias=b_gate[l])

    lam = (jnp.exp(jnp.sum(lambda_q1[l].astype(F32) * lambda_k1[l].astype(F32)))
           - jnp.exp(jnp.sum(lambda_q2[l].astype(F32) * lambda_k2[l].astype(F32))) + LAM_INIT)
    o_a = _diff_attention(qkv.reshape(b, s_len, qkv_cols), rel_table, lam, q_gain[l], k_gain[l], sub_gain[l])
    o_b = _lru_branch(lru.reshape(b, s_len, 2 * lru_w), conv_w[l], conv_b[l], lru_wa[l], lru_ba[l],
                      lru_wx[l], lru_bx[l], lru_lambda[l])

    mixed = _merge(o_a.reshape(t_tok, -1), o_b.reshape(t_tok, lru_w),
                   w_branch_a[l].astype(BF16), w_branch_b[l].astype(BF16), gates)
    x1, xn, ti, tg, pos, cnt = _outproj_router(mixed, w_out[l].astype(BF16), xt, ln_ffn[l], w_router[l], b_router[l])
    y = _moe(x1, xn, ti, tg, pos, cnt, w1[l], b1[l], w2[l], b2[l])
    return y.reshape(b, s_len, d)
```

```python
import functools
import math

import jax
import jax.numpy as jnp
from jax import lax
from jax.experimental import pallas as pl
from jax.experimental.pallas import tpu as pltpu

F32 = jnp.float32
BF16 = jnp.bfloat16

CHUNK = 64
RMS_EPS = 1e-6
A_HEADS = 8
A_QK_DIM = 64
A_V_DIM = 2 * A_QK_DIM
LRU_BLOCKS = 8
CONV_WIDTH = 4
LRU_C = 8.0
REL_BUCKETS = 32
REL_MAX_DIST = 128
TOP_K = 4
SWIGLU_LIMIT = 7.0
SWIGLU_ALPHA = 1.702
LAM_INIT = 0.8 - 0.6 * math.exp(-0.3 * 0)
LOG2E = 1.4426950408889634

LANES = 128
SUBLANES = 8
MXU_DIM = 256
NEG_BIG = -1e30
NEG_SEL = -3e38

ATT_Q_BLOCK = 512
ATT_ROW_GROUP = 128
MOE_ROWS = 512
GATHER_ROWS = 512
COMBINE_TOKENS = 128


def _tile(n, pref):
    t = min(n, pref)
    assert n % t == 0, (n, pref)
    return t


def _params(semantics, vmem_mib):
    return pltpu.CompilerParams(dimension_semantics=semantics, vmem_limit_bytes=vmem_mib * 1024 * 1024)


def _rmsnorm_kernel(x_ref, g_ref, o_ref):
    x = x_ref[...]
    y = x * lax.rsqrt(jnp.mean(x * x, axis=-1, keepdims=True) + RMS_EPS)
    o_ref[...] = (y * g_ref[...]).astype(o_ref.dtype)


def _rmsnorm(x, g, out_dtype):
    m, d = x.shape
    tm = _tile(m, 512)
    return pl.pallas_call(
        _rmsnorm_kernel,
        grid=(m // tm,),
        in_specs=[pl.BlockSpec((tm, d), lambda i: (i, 0)), pl.BlockSpec((1, d), lambda i: (0, 0))],
        out_specs=pl.BlockSpec((tm, d), lambda i: (i, 0)),
        out_shape=jax.ShapeDtypeStruct((m, d), out_dtype),
        compiler_params=_params(("parallel",), 32),
        name="rmsnorm",
    )(x, g.reshape(1, d))


def _proj_kernel(a_ref, w_ref, *rest, sigmoid_bias):
    if sigmoid_bias:
        b_ref, o_ref = rest
    else:
        (o_ref,) = rest
    acc = jnp.dot(a_ref[...], w_ref[...], preferred_element_type=F32)
    if sigmoid_bias:
        acc = jax.nn.sigmoid(acc + b_ref[...])
    o_ref[...] = acc.astype(o_ref.dtype)


def _proj(a, w, out_dtype, bias=None, tm=1024, tn=1024):
    m, k = a.shape
    n = w.shape[1]
    tm, tn = _tile(m, tm), _tile(n, tn)
    in_specs = [pl.BlockSpec((tm, k), lambda i, j: (i, 0)), pl.BlockSpec((k, tn), lambda i, j: (0, j))]
    args = [a, w]
    if bias is not None:
        in_specs.append(pl.BlockSpec((1, tn), lambda i, j: (0, j)))
        args.append(bias.reshape(1, n))
    return pl.pallas_call(
        functools.partial(_proj_kernel, sigmoid_bias=bias is not None),
        grid=(m // tm, n // tn),
        in_specs=in_specs,
        out_specs=pl.BlockSpec((tm, tn), lambda i, j: (i, j)),
        out_shape=jax.ShapeDtypeStruct((m, n), out_dtype),
        compiler_params=_params(("parallel", "parallel"), 48),
        name="proj",
    )(*args)


def _t5_bucket(rel):
    nb = REL_BUCKETS // 2
    max_exact = nb // 2
    ret = jnp.where(rel > 0, nb, 0)
    n = jnp.abs(rel)
    nf = jnp.maximum(n, 1).astype(F32)
    large = max_exact + (jnp.log(nf / max_exact) / math.log(REL_MAX_DIST / max_exact)
                         * (nb - max_exact)).astype(jnp.int32)
    large = jnp.minimum(large, nb - 1)
    return ret + jnp.where(n < max_exact, n, large)


def _near_bias(rel_table, t):
    assert t + 1 >= REL_MAX_DIST
    table = rel_table.astype(F32)
    r = jnp.arange(t, dtype=jnp.int32)[:, None]
    c = jnp.arange(t, dtype=jnp.int32)[None, :]

    def lookup(bucket):
        out = jnp.zeros((A_HEADS,) + bucket.shape, F32)
        for bkt in range(REL_BUCKETS):
            out = jnp.where(bucket[None] == bkt, table[bkt][:, None, None], out)
        return out

    far = table[_t5_bucket(jnp.int32(-(t + 1)))][:, None, None]
    diag = lookup(_t5_bucket(c - r)) - far
    allowed = (c // CHUNK) <= (r // CHUNK)
    diag = jnp.where(allowed[None], diag * LOG2E, NEG_BIG)
    sub = (lookup(_t5_bucket(c - r - t)) - far) * LOG2E
    return jnp.stack([jnp.zeros_like(diag), sub, diag], axis=1)


def _attn_kernel(lam_ref, q_ref, k_ref, v_ref, nb_ref, qg_ref, kg_ref, sg_ref, o_ref,
                 kn_ref, vx_ref, m_ref, acc_ref, qs_ref, s_ref, *, t, s_len, k_chunk, rg):
    qi = pl.program_id(2)
    hw = 2 * A_QK_DIM
    rows = 2 * t
    lo = lax.broadcasted_iota(jnp.int32, (1, hw), 1) < A_QK_DIM

    def qk_norm(x, g):
        sq = x * x
        s_lo = jnp.sum(jnp.where(lo, sq, 0.0), axis=-1, keepdims=True)
        s_hi = jnp.sum(jnp.where(lo, 0.0, sq), axis=-1, keepdims=True)
        ms = jnp.where(lo, s_lo, s_hi) * (1.0 / A_QK_DIM)
        return x * lax.rsqrt(ms + RMS_EPS) * g

    @pl.when(qi == 0)
    def _():
        def body(c, carry):
            r0 = pl.multiple_of(c * k_chunk, k_chunk)
            kk = k_ref[0, pl.ds(r0, k_chunk), :].astype(F32)
            kn_ref[pl.ds(r0, k_chunk), :] = qk_norm(kk, kg_ref[...]).astype(BF16)
            vx_ref[pl.ds(r0, k_chunk), 0:hw] = v_ref[0, pl.ds(r0, k_chunk), :]
            vx_ref[pl.ds(r0, k_chunk), hw:2 * hw] = jnp.ones((k_chunk, hw), BF16)
            return carry
        lax.fori_loop(0, s_len // k_chunk, body, 0)

    q = qk_norm(q_ref[0].astype(F32), qg_ref[...]) * (A_QK_DIM ** -0.5 * LOG2E)
    qs = jnp.concatenate([jnp.where(lo, q, 0.0), jnp.where(lo, 0.0, q)], axis=0).astype(BF16)

    m_ref[...] = jnp.full(m_ref.shape, NEG_BIG, F32)
    acc_ref[...] = jnp.zeros(acc_ref.shape, F32)

    qs_ref[...] = qs

    groups = [slice(g * rg, (g + 1) * rg) for g in range(rows // rg)]

    def logits(jb, slot):
        r0 = pl.multiple_of(jb * t, t)
        kj = kn_ref[pl.ds(r0, t), :]
        tile = jnp.clip(jb - (qi - 2), 0, 2)
        for rs in groups:
            s = lax.dot_general(qs_ref[rs, :], kj, (((1,), (1,)), ((), ())), preferred_element_type=F32)
            b0 = rs.start % t
            s_ref[slot, rs, :] = s + nb_ref[0, tile, b0:b0 + rg, :]

    def accumulate(jb, slot):
        r0 = pl.multiple_of(jb * t, t)
        vj = vx_ref[pl.ds(r0, t), :]
        for rs in groups:
            s = s_ref[slot, rs, :]
            chunks = [s[:, c * LANES:(c + 1) * LANES] for c in range(t // LANES)]
            mc = chunks[0]
            for ch in chunks[1:]:
                mc = jnp.maximum(mc, ch)
            m_prev = m_ref[rs, :]
            m_new = jnp.maximum(m_prev, jnp.max(mc, axis=-1, keepdims=True))
            alpha = jnp.exp2(m_prev - m_new)
            p = jnp.concatenate([jnp.exp2(ch - m_new) for ch in chunks], axis=1).astype(BF16)
            pv = jnp.dot(p, vj, preferred_element_type=F32)
            acc_ref[rs, :] = jnp.concatenate([alpha, alpha], axis=1) * acc_ref[rs, :] + pv
            m_ref[rs, :] = m_new

    n_blocks = qi + 1
    logits(0, 0)

    def pair_body(jp, carry):
        j = 2 * jp
        accumulate(j, 0)
        logits(j + 1, 1)
        accumulate(j + 1, 1)
        logits(jnp.minimum(j + 2, qi), 0)
        return carry
    lax.fori_loop(0, n_blocks // 2, pair_body, 0)

    @pl.when(n_blocks % 2 == 1)
    def _():
        accumulate(qi, 0)

    lam = lam_ref[0]
    acc = acc_ref[...]
    o = acc[:t, :hw] / acc[:t, hw:] - lam * (acc[t:, :hw] / acc[t:, hw:])
    o = o * lax.rsqrt(jnp.mean(o * o, axis=-1, keepdims=True) + RMS_EPS) * sg_ref[...]
    o_ref[0] = (o * (1.0 - LAM_INIT)).astype(o_ref.dtype)


def _diff_attention(qkv, rel_table, lam, q_gain, k_gain, sub_gain):
    b, s_len, _ = qkv.shape
    t = _tile(s_len, ATT_Q_BLOCK)
    hw = 2 * A_QK_DIM
    nb = _near_bias(rel_table, t)
    tile2 = lambda g: jnp.concatenate([g, g]).reshape(1, hw).astype(F32)
    kern = functools.partial(_attn_kernel, t=t, s_len=s_len, k_chunk=_tile(s_len, 512), rg=_tile(t, ATT_ROW_GROUP))
    return pl.pallas_call(
        kern,
        grid=(b, A_HEADS, s_len // t),
        in_specs=[
            pl.BlockSpec(memory_space=pltpu.SMEM),
            pl.BlockSpec((1, t, hw), lambda bi, h, qi: (bi, qi, h)),
            pl.BlockSpec((1, s_len, hw), lambda bi, h, qi: (bi, 0, A_HEADS + h)),
            pl.BlockSpec((1, s_len, hw), lambda bi, h, qi: (bi, 0, 2 * A_HEADS + h)),
            pl.BlockSpec((1, 3, t, t), lambda bi, h, qi: (h, 0, 0, 0)),
            pl.BlockSpec((1, hw), lambda bi, h, qi: (0, 0)),
            pl.BlockSpec((1, hw), lambda bi, h, qi: (0, 0)),
            pl.BlockSpec((1, hw), lambda bi, h, qi: (0, 0)),
        ],
        out_specs=pl.BlockSpec((1, t, hw), lambda bi, h, qi: (bi, qi, h)),
        out_shape=jax.ShapeDtypeStruct((b, s_len, A_HEADS * hw), BF16),
        scratch_shapes=[
            pltpu.VMEM((s_len, hw), BF16),
            pltpu.VMEM((s_len, 2 * hw), BF16),
            pltpu.VMEM((2 * t, hw), F32),
            pltpu.VMEM((2 * t, 2 * hw), F32),
            pltpu.VMEM((2 * t, hw), BF16),
            pltpu.VMEM((2, 2 * t, t), F32),
        ],
        compiler_params=_params(("arbitrary", "arbitrary", "arbitrary"), 48),
        name="diff_attention",
    )(lam.reshape(1).astype(F32), qkv, qkv, qkv, nb, tile2(q_gain), tile2(k_gain),
      sub_gain.reshape(1, hw).astype(F32))


def _lru_kernel(xl_ref, gl_ref, cw_ref, cb_ref, wa_ref, ba_ref, wx_ref, bx_ref, c_ref, o_ref,
                xbuf, a_s, u_s, h_s, hc, *, ts, width):
    i = pl.program_id(1)
    halo = SUBLANES
    bd = width // LRU_BLOCKS

    @pl.when(i == 0)
    def _():
        xbuf[0:halo, :] = jnp.zeros((halo, width), F32)
        hc[...] = jnp.zeros(hc.shape, F32)

    x = xl_ref[0]
    xbuf[halo:halo + ts, :] = x
    xr = cb_ref[...] + cw_ref[0:1, :] * xbuf[pl.ds(halo - 3, ts), :]
    for j in range(1, CONV_WIDTH):
        xr = xr + cw_ref[j:j + 1, :] * xbuf[pl.ds(halo - 3 + j, ts), :]
    xbuf[0:halo, :] = x[ts - halo:, :]

    for g in range(LRU_BLOCKS):
        sl = slice(g * bd, (g + 1) * bd)
        xg = xr[:, sl]
        xb = xg.astype(BF16)
        r = jax.nn.sigmoid(jnp.dot(xb, wa_ref[g], preferred_element_type=F32) + ba_ref[:, sl])
        gi = jax.nn.sigmoid(jnp.dot(xb, wx_ref[g], preferred_element_type=F32) + bx_ref[:, sl])
        log_a = r * c_ref[:, sl]
        a = jnp.exp(log_a)
        a_s[:, sl] = a
        u_s[:, sl] = jnp.sqrt(-jnp.tanh(log_a) * (1.0 + a * a)) * (gi * xg)

    row = lax.broadcasted_iota(jnp.int32, (SUBLANES, width), 0)

    def body(gidx, h):
        r0 = pl.multiple_of(gidx * SUBLANES, SUBLANES)
        a = a_s[pl.ds(r0, SUBLANES), :]
        u = u_s[pl.ds(r0, SUBLANES), :]
        for d in (1, 2, 4):
            keep = row >= d
            u = jnp.where(keep, a * pltpu.roll(u, d, 0) + u, u)
            a = jnp.where(keep, a * pltpu.roll(a, d, 0), a)
        hr = a * h + u
        h_s[pl.ds(r0, SUBLANES), :] = hr
        return hr[SUBLANES - 1:SUBLANES, :]

    hc[...] = lax.fori_loop(0, ts // SUBLANES, body, hc[...], unroll=2)
    o_ref[0] = (h_s[...] * jax.nn.gelu(gl_ref[0])).astype(o_ref.dtype)


def _lru_branch(lru, conv_w, conv_b, wa, ba, wx, bx, lam_param):
    b, s_len, c2 = lru.shape
    width = c2 // 2
    ts = _tile(s_len, 256)
    bd = width // LRU_BLOCKS
    c_vec = (-LRU_C * jax.nn.softplus(-lam_param.astype(F32))).reshape(1, width)
    row = lambda v: v.reshape(1, width).astype(F32)
    const2 = lambda shape: pl.BlockSpec(shape, lambda bi, i: (0,) * len(shape))
    kern = functools.partial(_lru_kernel, ts=ts, width=width)
    return pl.pallas_call(
        kern,
        grid=(b, s_len // ts),
        in_specs=[
            pl.BlockSpec((1, ts, width), lambda bi, i: (bi, i, 0)),
            pl.BlockSpec((1, ts, width), lambda bi, i: (bi, i, 1)),
            const2((CONV_WIDTH, width)), const2((1, width)),
            const2((LRU_BLOCKS, bd, bd)), const2((1, width)),
            const2((LRU_BLOCKS, bd, bd)), const2((1, width)),
            const2((1, width)),
        ],
        out_specs=pl.BlockSpec((1, ts, width), lambda bi, i: (bi, i, 0)),
        out_shape=jax.ShapeDtypeStruct((b, s_len, width), BF16),
        scratch_shapes=[
            pltpu.VMEM((ts + SUBLANES, width), F32),
            pltpu.VMEM((ts, width), F32),
            pltpu.VMEM((ts, width), F32),
            pltpu.VMEM((ts, width), F32),
            pltpu.VMEM((1, width), F32),
        ],
        compiler_params=_params(("arbitrary", "arbitrary"), 40),
        name="rg_lru",
    )(lru, lru, conv_w.astype(F32), row(conv_b), wa.astype(BF16), row(ba), wx.astype(BF16), row(bx), c_vec)


def _merge_kernel(oa_ref, ob_ref, wa_ref, wb_ref, g0_ref, g1_ref, o_ref):
    ya = jnp.dot(oa_ref[...], wa_ref[...], preferred_element_type=F32)
    yb = jnp.dot(ob_ref[...], wb_ref[...], preferred_element_type=F32)
    o_ref[...] = (g0_ref[...].astype(F32) * ya + g1_ref[...].astype(F32) * yb).astype(o_ref.dtype)


def _merge(o_a, o_b, w_a, w_b, gates):
    m, ka = o_a.shape
    kb = o_b.shape[1]
    d = w_a.shape[1]
    tm, tn = _tile(m, 512), _tile(d, 1024)
    nj = d // tn
    return pl.pallas_call(
        _merge_kernel,
        grid=(m // tm, nj),
        in_specs=[
            pl.BlockSpec((tm, ka), lambda i, j: (i, 0)),
            pl.BlockSpec((tm, kb), lambda i, j: (i, 0)),
            pl.BlockSpec((ka, tn), lambda i, j: (0, j)),
            pl.BlockSpec((kb, tn), lambda i, j: (0, j)),
            pl.BlockSpec((tm, tn), lambda i, j: (i, j)),
            pl.BlockSpec((tm, tn), lambda i, j: (i, j + nj)),
        ],
        out_specs=pl.BlockSpec((tm, tn), lambda i, j: (i, j)),
        out_shape=jax.ShapeDtypeStruct((m, d), BF16),
        compiler_params=_params(("parallel", "parallel"), 40),
        name="branch_merge",
    )(o_a, o_b, w_a, w_b, gates, gates)


def _outproj_kernel(mx_ref, wo_ref, x_ref, g_ref, x1_ref, xn_ref):
    x1 = x_ref[...] + jnp.dot(mx_ref[...], wo_ref[...], preferred_element_type=F32)
    x1_ref[...] = x1
    xn_ref[...] = x1 * lax.rsqrt(jnp.mean(x1 * x1, axis=-1, keepdims=True) + RMS_EPS) * g_ref[...]


def _outproj(mixed, w_out, x, ln_ffn):
    m, d = x.shape
    tm = _tile(m, 512)
    row_blk = pl.BlockSpec((tm, d), lambda i: (i, 0))
    return pl.pallas_call(
        _outproj_kernel,
        grid=(m // tm,),
        in_specs=[row_blk, pl.BlockSpec((d, d), lambda i: (0, 0)), row_blk, pl.BlockSpec((1, d), lambda i: (0, 0))],
        out_specs=[row_blk, row_blk],
        out_shape=[jax.ShapeDtypeStruct((m, d), F32), jax.ShapeDtypeStruct((m, d), F32)],
        compiler_params=_params(("parallel",), 56),
        name="outproj",
    )(mixed, w_out, x, ln_ffn.reshape(1, d).astype(F32))


def _router_kernel(xn_ref, w2_ref, wh_ref, br_ref, ti_ref, tg_ref, pos_ref, cnt_ref, carry, *, tm, sub):
    i = pl.program_id(0)

    @pl.when(i == 0)
    def _():
        carry[...] = jnp.zeros(carry.shape, F32)

    lane = lax.broadcasted_iota(jnp.int32, (sub, LANES), 1)
    rr = lax.broadcasted_iota(jnp.int32, (sub, sub), 0)
    cc = lax.broadcasted_iota(jnp.int32, (sub, sub), 1)
    below = (cc < rr).astype(BF16)

    for c in range(tm // sub):
        rs = slice(c * sub, (c + 1) * sub)
        xn = xn_ref[rs, :]
        xh = xn.astype(BF16)
        xl = (xn - xh.astype(F32)).astype(BF16)
        hi = jnp.dot(xh, w2_ref[...], preferred_element_type=F32)
        lo = jnp.dot(xl, wh_ref[...], preferred_element_type=F32)
        rest = hi[:, :LANES] + hi[:, LANES:] + lo + br_ref[...]

        vals, idxs = [], []
        for _ in range(TOP_K):
            mx = jnp.max(rest, axis=-1, keepdims=True)
            ix = jnp.min(jnp.where(rest == mx, lane, LANES), axis=-1, keepdims=True)
            vals.append(mx)
            idxs.append(ix)
            rest = jnp.where(lane == ix, NEG_SEL, rest)
        exps = [jnp.exp(v - vals[0]) for v in vals]
        den = exps[0]
        for e in exps[1:]:
            den = den + e
        ti = jnp.zeros((sub, LANES), jnp.int32)
        tg = jnp.zeros((sub, LANES), F32)
        sel = jnp.zeros((sub, LANES), F32)
        for k in range(TOP_K):
            ti = jnp.where(lane == k, idxs[k], ti)
            tg = jnp.where(lane == k, exps[k] / den, tg)
            sel = jnp.where(lane == idxs[k], 1.0, sel)
        ti_ref[rs, :] = ti
        tg_ref[rs, :] = tg

        pos = jnp.dot(below, sel.astype(BF16), preferred_element_type=F32) + carry[...]
        pos4 = jnp.zeros((sub, LANES), F32)
        for k in range(TOP_K):
            pk = jnp.sum(jnp.where(lane == idxs[k], pos, 0.0), axis=-1, keepdims=True)
            pos4 = jnp.where(lane == k, pk, pos4)
        pos_ref[rs, :] = pos4.astype(jnp.int32)
        carry[...] = carry[...] + jnp.sum(sel, axis=0, keepdims=True)
    cnt_ref[...] = carry[...].astype(jnp.int32)


def _router(xn, w_router, b_router):
    m, d = xn.shape
    e = w_router.shape[1]
    assert e <= LANES
    tm = _tile(m, 1024)
    wr = jnp.zeros((d, LANES), F32).at[:, :e].set(w_router.astype(F32))
    wh = wr.astype(BF16)
    wl = (wr - wh.astype(F32)).astype(BF16)
    br = jnp.full((1, LANES), NEG_BIG, F32).at[0, :e].set(b_router.astype(F32))
    row_blk = lambda w: pl.BlockSpec((tm, w), lambda i: (i, 0))
    const = lambda shape: pl.BlockSpec(shape, lambda i: (0, 0))
    return pl.pallas_call(
        functools.partial(_router_kernel, tm=tm, sub=_tile(tm, 256)),
        grid=(m // tm,),
        in_specs=[row_blk(d), const((d, 2 * LANES)), const((d, LANES)), const((1, LANES))],
        out_specs=[row_blk(LANES), row_blk(LANES), row_blk(LANES), const((1, LANES))],
        out_shape=[
            jax.ShapeDtypeStruct((m, LANES), jnp.int32),
            jax.ShapeDtypeStruct((m, LANES), F32),
            jax.ShapeDtypeStruct((m, LANES), jnp.int32),
            jax.ShapeDtypeStruct((1, LANES), jnp.int32),
        ],
        scratch_shapes=[pltpu.VMEM((1, LANES), F32)],
        compiler_params=_params(("arbitrary",), 40),
        name="router",
    )(xn, jnp.concatenate([wh, wl], axis=1), wh, br)


def _gather_kernel(nv_ref, idx_ref, nxt_ref, src_ref, *rest, rows, groups, weighted):
    if weighted:
        g_ref, base_ref, o_ref, buf, sem = rest
    else:
        o_ref, buf, sem = rest
    i = pl.program_id(0)
    n = nv_ref[0]
    total = rows * groups

    def row_copy(ref, r, slot):
        return pltpu.make_async_copy(src_ref.at[pl.ds(ref[0, 0, r], 1), :],
                                     buf.at[slot, pl.ds(r, 1), :], sem.at[slot])

    def issue(ref, slot):
        def body(r, carry):
            row_copy(ref, r, slot).start()
            return carry
        lax.fori_loop(0, total, body, 0, unroll=8)

    @pl.when(jnp.logical_and(i == 0, n > 0))
    def _():
        issue(idx_ref, 0)

    @pl.when(i + 1 < n)
    def _():
        issue(nxt_ref, (i + 1) % 2)

    slot = i % 2

    @pl.when(i < n)
    def _():
        def wait_body(r, carry):
            row_copy(idx_ref, r, slot).wait()
            return carry
        lax.fori_loop(0, total, wait_body, 0, unroll=8)

        if weighted:
            lane = lax.broadcasted_iota(jnp.int32, (rows, LANES), 1)
            g = g_ref[...]
            acc = base_ref[...]
            for k in range(groups):
                gk = jnp.sum(jnp.where(lane == k, g, 0.0), axis=-1, keepdims=True)
                acc = acc + gk * buf[slot, pl.ds(k * rows, rows), :]
            o_ref[...] = acc.astype(o_ref.dtype)
        else:
            o_ref[...] = buf[slot].astype(o_ref.dtype)

    @pl.when(i >= n)
    def _():
        o_ref[...] = jnp.zeros(o_ref.shape, o_ref.dtype)


def _gather_rows(src, idx, n_valid, rows, out_dtype, gates=None, base=None):
    steps, _, total = idx.shape
    groups = total // rows
    d = src.shape[1]
    weighted = gates is not None
    in_specs = [
        pl.BlockSpec(memory_space=pltpu.SMEM),
        pl.BlockSpec((1, 1, total), lambda i: (i, 0, 0), memory_space=pltpu.SMEM),
        pl.BlockSpec((1, 1, total), lambda i: (jnp.minimum(i + 1, steps - 1), 0, 0), memory_space=pltpu.SMEM),
        pl.BlockSpec(memory_space=pl.ANY),
    ]
    args = [n_valid, idx, idx, src]
    if weighted:
        in_specs += [pl.BlockSpec((rows, LANES), lambda i: (i, 0)), pl.BlockSpec((rows, d), lambda i: (i, 0))]
        args += [gates, base]
    return pl.pallas_call(
        functools.partial(_gather_kernel, rows=rows, groups=groups, weighted=weighted),
        grid=(steps,),
        in_specs=in_specs,
        out_specs=pl.BlockSpec((rows, d), lambda i: (i, 0)),
        out_shape=jax.ShapeDtypeStruct((steps * rows, d), out_dtype),
        scratch_shapes=[pltpu.VMEM((2, total, d), F32), pltpu.SemaphoreType.DMA((2,))],
        compiler_params=_params(("arbitrary",), 48),
        name="combine_rows" if weighted else "gather_rows",
    )(*args)


def _zero_unused_block(nu_ref, o_ref):
    @pl.when(pl.program_id(1) >= nu_ref[0])
    def _():
        o_ref[...] = jnp.zeros(o_ref.shape, o_ref.dtype)


def _moe1_kernel(be_ref, nu_ref, first_ref, x_ref, w_ref, bg_ref, bl_ref, o_ref, wp_ref, *, tn):
    i = pl.program_id(1)
    groups = tn // MXU_DIM
    half = MXU_DIM // 2
    _zero_unused_block(nu_ref, o_ref)

    @pl.when(first_ref[i] == 1)
    def _():
        rr = lax.broadcasted_iota(jnp.int32, (MXU_DIM, MXU_DIM), 0)
        cc = lax.broadcasted_iota(jnp.int32, (MXU_DIM, MXU_DIM), 1)
        src = jnp.where(cc < half, 2 * cc, 2 * (cc - half) + 1)
        perm = (rr == src).astype(BF16)
        for g in range(groups):
            sl = slice(g * MXU_DIM, (g + 1) * MXU_DIM)
            wt = w_ref[0, :, sl].astype(BF16)
            wp_ref[:, sl] = jnp.dot(wt, perm, preferred_element_type=F32).astype(BF16)

    @pl.when(i < nu_ref[0])
    def _():
        h = jnp.dot(x_ref[...], wp_ref[...], preferred_element_type=F32)
        for g in range(groups):
            fs = slice(g * half, (g + 1) * half)
            hg = h[:, g * MXU_DIM:g * MXU_DIM + half] + bg_ref[0, :, fs]
            hl = h[:, g * MXU_DIM + half:(g + 1) * MXU_DIM] + bl_ref[0, :, fs]
            glu = jnp.minimum(hg, SWIGLU_LIMIT)
            lin = jnp.clip(hl, -SWIGLU_LIMIT, SWIGLU_LIMIT)
            o_ref[:, fs] = (glu * jax.nn.sigmoid(SWIGLU_ALPHA * glu) * (lin + 1.0)).astype(o_ref.dtype)


def _moe2_kernel(be_ref, nu_ref, first_ref, a_ref, w_ref, b_ref, o_ref, wb_ref):
    i = pl.program_id(1)
    _zero_unused_block(nu_ref, o_ref)

    @pl.when(first_ref[i] == 1)
    def _():
        wb_ref[...] = w_ref[0].astype(BF16)

    @pl.when(i < nu_ref[0])
    def _():
        o_ref[...] = jnp.dot(a_ref[...], wb_ref[...], preferred_element_type=F32) + b_ref[0]


def _grouped(kernel, name, rows_in, weight, biases, sched, bm, tn, out_cols_per_tile, out_dtype):
    p, k = rows_in.shape
    n = weight.shape[2]
    nblk = p // bm
    tb = out_cols_per_tile
    blk = lambda i, nu: jnp.maximum(jnp.minimum(i, nu[0] - 1), 0)
    grid_spec = pltpu.PrefetchScalarGridSpec(
        num_scalar_prefetch=3,
        grid=(n // tn, nblk),
        in_specs=[pl.BlockSpec((bm, k), lambda j, i, be, nu, fi: (blk(i, nu), 0)),
                  pl.BlockSpec((1, k, tn), lambda j, i, be, nu, fi: (be[i], 0, j))]
        + [pl.BlockSpec((1, 1, tb), lambda j, i, be, nu, fi: (be[i], 0, j))] * len(biases),
        out_specs=pl.BlockSpec((bm, tb), lambda j, i, be, nu, fi: (i, j)),
        scratch_shapes=[pltpu.VMEM((k, tn), BF16)],
    )
    return pl.pallas_call(
        kernel,
        grid_spec=grid_spec,
        out_shape=jax.ShapeDtypeStruct((p, (n // tn) * tb), out_dtype),
        compiler_params=_params(("arbitrary", "arbitrary"), 60),
        name=name,
    )(*sched, rows_in, weight, *biases)


def _moe(x1, xn, ti, tg, pos, cnt, w1, b1, w2, b2):
    t_tok, d = x1.shape
    n_exp, _, f2 = w1.shape
    f = f2 // 2
    bm = MOE_ROWS
    assert (t_tok * TOP_K) % bm == 0
    nblk = t_tok * TOP_K // bm + n_exp
    p = nblk * bm

    ti4 = ti[:, :TOP_K]
    counts = cnt[0, :n_exp]
    padded = (counts + bm - 1) // bm * bm
    pad_ends = jnp.cumsum(padded)
    pad_starts = pad_ends - padded
    onehot = ti4[:, :, None] == jnp.arange(n_exp, dtype=jnp.int32)[None, None, :]
    dest4 = jnp.sum(jnp.where(onehot, pad_starts[None, None, :], 0), axis=-1) + pos[:, :TOP_K]
    n_used = (pad_ends[-1] // bm).astype(jnp.int32).reshape(1)
    blk_start = jnp.arange(nblk, dtype=jnp.int32) * bm
    blk_start = jnp.minimum(blk_start, pad_ends[-1] - bm)
    blk_expert = jnp.sum(blk_start[:, None] >= pad_ends[None, :], axis=1).astype(jnp.int32)
    blk_expert = jnp.minimum(blk_expert, n_exp - 1)
    first = jnp.concatenate([jnp.ones((1,), jnp.int32), (blk_expert[1:] != blk_expert[:-1]).astype(jnp.int32)])
    sched = (blk_expert, n_used, first)
    tok = jnp.broadcast_to(jnp.arange(t_tok, dtype=jnp.int32)[:, None], (t_tok, TOP_K))
    row_tok = (jnp.arange(p, dtype=jnp.int32) % t_tok).at[dest4.reshape(-1)].set(tok.reshape(-1), unique_indices=True)

    gr = _tile(bm, GATHER_ROWS)
    assert bm % gr == 0
    xg = _gather_rows(xn, row_tok.reshape(p // gr, 1, gr), n_used * (bm // gr), gr, BF16)

    b1g = b1[:, 0::2].reshape(n_exp, 1, f).astype(F32)
    b1l = b1[:, 1::2].reshape(n_exp, 1, f).astype(F32)
    tn1 = _tile(f2, 2048)
    assert tn1 % MXU_DIM == 0
    act = _grouped(functools.partial(_moe1_kernel, tn=tn1), "expert_up", xg, w1, [b1g, b1l], sched,
                   bm, tn1, tn1 // 2, BF16)
    tn2 = _tile(d, 2048)
    out = _grouped(_moe2_kernel, "expert_down", act, w2, [b2.reshape(n_exp, 1, d).astype(F32)], sched,
                   bm, tn2, tn2, F32)

    tc = _tile(t_tok, COMBINE_TOKENS)
    steps = t_tok // tc
    idx = jnp.transpose(dest4.reshape(steps, tc, TOP_K), (0, 2, 1)).reshape(steps, 1, TOP_K * tc)
    return _gather_rows(out, idx, jnp.full((1,), steps, jnp.int32), tc, F32, gates=tg, base=x1)


def kernel(x, ln_mix, w_in, b_gate, q_gain, k_gain, lambda_q1, lambda_k1, lambda_q2, lambda_k2, sub_gain, rel_table, conv_w, conv_b, lru_wa, lru_ba, lru_wx, lru_bx, lru_lambda, w_branch_a, w_branch_b, w_out, ln_ffn, w_router, b_router, w1, b1, w2, b2):
    b, s_len, d = x.shape
    t_tok = b * s_len
    depth = ln_mix.shape[0]
    assert depth == 1
    q_cols = A_HEADS * 2 * A_QK_DIM
    qkv_cols = 2 * q_cols + A_HEADS * A_V_DIM
    lru_w = d // 2
    xt = x.reshape(t_tok, d)
    l = 0

    hn = _rmsnorm(xt, ln_mix[l], BF16)
    w_in_b = w_in[l].astype(BF16)
    qkv = _proj(hn, w_in_b[:, :qkv_cols], BF16)
    lru = _proj(hn, w_in_b[:, qkv_cols:qkv_cols + 2 * lru_w], F32)
    gates = _proj(hn, w_in_b[:, qkv_cols + 2 * lru_w:], BF16, bias=b_gate[l])

    lam = (jnp.exp(jnp.sum(lambda_q1[l].astype(F32) * lambda_k1[l].astype(F32)))
           - jnp.exp(jnp.sum(lambda_q2[l].astype(F32) * lambda_k2[l].astype(F32))) + LAM_INIT)
    o_a = _diff_attention(qkv.reshape(b, s_len, qkv_cols), rel_table, lam, q_gain[l], k_gain[l], sub_gain[l])
    o_b = _lru_branch(lru.reshape(b, s_len, 2 * lru_w), conv_w[l], conv_b[l], lru_wa[l], lru_ba[l],
                      lru_wx[l], lru_bx[l], lru_lambda[l])

    mixed = _merge(o_a.reshape(t_tok, -1), o_b.reshape(t_tok, lru_w),
                   w_branch_a[l].astype(BF16), w_branch_b[l].astype(BF16), gates)
    x1, xn = _outproj(mixed, w_out[l].astype(BF16), xt, ln_ffn[l])
    ti, tg, pos, cnt = _router(xn, w_router[l], b_router[l])
    y = _moe(x1, xn, ti, tg, pos, cnt, w1[l], b1[l], w2[l], b2[l])
    return y.reshape(b, s_len, d)
```

```python
import functools
import math

import jax
import jax.numpy as jnp
from jax import lax
from jax.experimental import pallas as pl
from jax.experimental.pallas import tpu as pltpu

F32 = jnp.float32
BF16 = jnp.bfloat16

CHUNK = 64
RMS_EPS = 1e-6
A_HEADS = 8
A_QK_DIM = 64
A_V_DIM = 2 * A_QK_DIM
LRU_BLOCKS = 8
CONV_WIDTH = 4
LRU_C = 8.0
REL_BUCKETS = 32
REL_MAX_DIST = 128
TOP_K = 4
SWIGLU_LIMIT = 7.0
SWIGLU_ALPHA = 1.702
LAM_INIT = 0.8 - 0.6 * math.exp(-0.3 * 0)
LOG2E = 1.4426950408889634

LANES = 128
SUBLANES = 8
MXU_DIM = 256
NEG_BIG = -1e30
NEG_SEL = -3e38

ATT_Q_BLOCK = 512
ATT_ROW_GROUP = 128
MOE_ROWS = 512
GATHER_ROWS = 512
COMBINE_TOKENS = 128


def _tile(n, pref):
    t = min(n, pref)
    assert n % t == 0, (n, pref)
    return t


def _params(semantics, vmem_mib):
    return pltpu.CompilerParams(dimension_semantics=semantics, vmem_limit_bytes=vmem_mib * 1024 * 1024)


def _rows_out_copy(step, slot, buf, o_hbm, sem, rows):
    return pltpu.make_async_copy(buf.at[slot], o_hbm.at[pl.ds(step * rows, rows), 0, :], sem.at[slot])


def _rows_out_reclaim(i, slot, buf, o_hbm, sem, rows):
    @pl.when(i >= 2)
    def _():
        _rows_out_copy(i - 2, slot, buf, o_hbm, sem, rows).wait()


def _rows_out_send(i, n, slot, buf, o_hbm, sem, rows):
    _rows_out_copy(i, slot, buf, o_hbm, sem, rows).start()

    @pl.when(i == n - 1)
    def _():
        @pl.when(n > 1)
        def _():
            _rows_out_copy(i - 1, 1 - slot, buf, o_hbm, sem, rows).wait()
        _rows_out_copy(i, slot, buf, o_hbm, sem, rows).wait()


def _rmsnorm_kernel(x_ref, g_ref, o_ref):
    x = x_ref[...]
    y = x * lax.rsqrt(jnp.mean(x * x, axis=-1, keepdims=True) + RMS_EPS)
    o_ref[...] = (y * g_ref[...]).astype(o_ref.dtype)


def _rmsnorm(x, g, out_dtype):
    m, d = x.shape
    tm = _tile(m, 512)
    return pl.pallas_call(
        _rmsnorm_kernel,
        grid=(m // tm,),
        in_specs=[pl.BlockSpec((tm, d), lambda i: (i, 0)), pl.BlockSpec((1, d), lambda i: (0, 0))],
        out_specs=pl.BlockSpec((tm, d), lambda i: (i, 0)),
        out_shape=jax.ShapeDtypeStruct((m, d), out_dtype),
        compiler_params=_params(("parallel",), 32),
        name="rmsnorm",
    )(x, g.reshape(1, d))


def _proj_kernel(a_ref, w_ref, *rest, sigmoid_bias):
    if sigmoid_bias:
        b_ref, o_ref = rest
    else:
        (o_ref,) = rest
    acc = jnp.dot(a_ref[...], w_ref[...], preferred_element_type=F32)
    if sigmoid_bias:
        acc = jax.nn.sigmoid(acc + b_ref[...])
    o_ref[...] = acc.astype(o_ref.dtype)


def _proj(a, w, out_dtype, bias=None, tm=1024, tn=1024):
    m, k = a.shape
    n = w.shape[1]
    tm, tn = _tile(m, tm), _tile(n, tn)
    in_specs = [pl.BlockSpec((tm, k), lambda i, j: (i, 0)), pl.BlockSpec((k, tn), lambda i, j: (0, j))]
    args = [a, w]
    if bias is not None:
        in_specs.append(pl.BlockSpec((1, tn), lambda i, j: (0, j)))
        args.append(bias.reshape(1, n))
    return pl.pallas_call(
        functools.partial(_proj_kernel, sigmoid_bias=bias is not None),
        grid=(m // tm, n // tn),
        in_specs=in_specs,
        out_specs=pl.BlockSpec((tm, tn), lambda i, j: (i, j)),
        out_shape=jax.ShapeDtypeStruct((m, n), out_dtype),
        compiler_params=_params(("parallel", "parallel"), 48),
        name="proj",
    )(*args)


def _t5_bucket(rel):
    nb = REL_BUCKETS // 2
    max_exact = nb // 2
    ret = jnp.where(rel > 0, nb, 0)
    n = jnp.abs(rel)
    nf = jnp.maximum(n, 1).astype(F32)
    large = max_exact + (jnp.log(nf / max_exact) / math.log(REL_MAX_DIST / max_exact)
                         * (nb - max_exact)).astype(jnp.int32)
    large = jnp.minimum(large, nb - 1)
    return ret + jnp.where(n < max_exact, n, large)


def _near_bias(rel_table, t):
    assert t + 1 >= REL_MAX_DIST
    table = rel_table.astype(F32)
    r = jnp.arange(t, dtype=jnp.int32)[:, None]
    c = jnp.arange(t, dtype=jnp.int32)[None, :]

    def lookup(bucket):
        out = jnp.zeros((A_HEADS,) + bucket.shape, F32)
        for bkt in range(REL_BUCKETS):
            out = jnp.where(bucket[None] == bkt, table[bkt][:, None, None], out)
        return out

    far = table[_t5_bucket(jnp.int32(-(t + 1)))][:, None, None]
    diag = lookup(_t5_bucket(c - r)) - far
    allowed = (c // CHUNK) <= (r // CHUNK)
    diag = jnp.where(allowed[None], diag * LOG2E, NEG_BIG)
    sub = (lookup(_t5_bucket(c - r - t)) - far) * LOG2E
    return jnp.stack([jnp.zeros_like(diag), sub, diag], axis=1)


def _attn_kernel(lam_ref, q_ref, k_ref, v_ref, nb_ref, qg_ref, kg_ref, sg_ref, o_ref,
                 kn_ref, vx_ref, m_ref, acc_ref, qs_ref, s_ref, *, t, s_len, k_chunk, rg):
    qi = pl.program_id(2)
    hw = 2 * A_QK_DIM
    rows = 2 * t
    lo = lax.broadcasted_iota(jnp.int32, (1, hw), 1) < A_QK_DIM

    def qk_norm(x, g):
        sq = x * x
        s_lo = jnp.sum(jnp.where(lo, sq, 0.0), axis=-1, keepdims=True)
        s_hi = jnp.sum(jnp.where(lo, 0.0, sq), axis=-1, keepdims=True)
        ms = jnp.where(lo, s_lo, s_hi) * (1.0 / A_QK_DIM)
        return x * lax.rsqrt(ms + RMS_EPS) * g

    @pl.when(qi == 0)
    def _():
        def body(c, carry):
            r0 = pl.multiple_of(c * k_chunk, k_chunk)
            kk = k_ref[0, pl.ds(r0, k_chunk), :].astype(F32)
            kn_ref[pl.ds(r0, k_chunk), :] = qk_norm(kk, kg_ref[...]).astype(BF16)
            vx_ref[pl.ds(r0, k_chunk), 0:hw] = v_ref[0, pl.ds(r0, k_chunk), :]
            vx_ref[pl.ds(r0, k_chunk), hw:2 * hw] = jnp.ones((k_chunk, hw), BF16)
            return carry
        lax.fori_loop(0, s_len // k_chunk, body, 0)

    q = qk_norm(q_ref[0].astype(F32), qg_ref[...]) * (A_QK_DIM ** -0.5 * LOG2E)
    qs = jnp.concatenate([jnp.where(lo, q, 0.0), jnp.where(lo, 0.0, q)], axis=0).astype(BF16)

    m_ref[...] = jnp.full(m_ref.shape, NEG_BIG, F32)
    acc_ref[...] = jnp.zeros(acc_ref.shape, F32)

    qs_ref[...] = qs

    groups = [slice(g * rg, (g + 1) * rg) for g in range(rows // rg)]

    def logits(jb, slot):
        r0 = pl.multiple_of(jb * t, t)
        kj = kn_ref[pl.ds(r0, t), :]
        tile = jnp.clip(jb - (qi - 2), 0, 2)
        for rs in groups:
            s = lax.dot_general(qs_ref[rs, :], kj, (((1,), (1,)), ((), ())), preferred_element_type=F32)
            b0 = rs.start % t
            s_ref[slot, rs, :] = s + nb_ref[0, tile, b0:b0 + rg, :]

    def accumulate(jb, slot):
        r0 = pl.multiple_of(jb * t, t)
        vj = vx_ref[pl.ds(r0, t), :]
        for rs in groups:
            s = s_ref[slot, rs, :]
            chunks = [s[:, c * LANES:(c + 1) * LANES] for c in range(t // LANES)]
            mc = chunks[0]
            for ch in chunks[1:]:
                mc = jnp.maximum(mc, ch)
            m_prev = m_ref[rs, :]
            m_new = jnp.maximum(m_prev, jnp.max(mc, axis=-1, keepdims=True))
            alpha = jnp.exp2(m_prev - m_new)
            p = jnp.concatenate([jnp.exp2(ch - m_new) for ch in chunks], axis=1).astype(BF16)
            pv = jnp.dot(p, vj, preferred_element_type=F32)
            acc_ref[rs, :] = jnp.concatenate([alpha, alpha], axis=1) * acc_ref[rs, :] + pv
            m_ref[rs, :] = m_new

    n_blocks = qi + 1
    logits(0, 0)

    def pair_body(jp, carry):
        j = 2 * jp
        accumulate(j, 0)
        logits(j + 1, 1)
        accumulate(j + 1, 1)
        logits(jnp.minimum(j + 2, qi), 0)
        return carry
    lax.fori_loop(0, n_blocks // 2, pair_body, 0)

    @pl.when(n_blocks % 2 == 1)
    def _():
        accumulate(qi, 0)

    lam = lam_ref[0]
    acc = acc_ref[...]
    o = acc[:t, :hw] / acc[:t, hw:] - lam * (acc[t:, :hw] / acc[t:, hw:])
    o = o * lax.rsqrt(jnp.mean(o * o, axis=-1, keepdims=True) + RMS_EPS) * sg_ref[...]
    o_ref[0] = (o * (1.0 - LAM_INIT)).astype(o_ref.dtype)


def _diff_attention(qkv, rel_table, lam, q_gain, k_gain, sub_gain):
    b, s_len, _ = qkv.shape
    t = _tile(s_len, ATT_Q_BLOCK)
    hw = 2 * A_QK_DIM
    nb = _near_bias(rel_table, t)
    tile2 = lambda g: jnp.concatenate([g, g]).reshape(1, hw).astype(F32)
    kern = functools.partial(_attn_kernel, t=t, s_len=s_len, k_chunk=_tile(s_len, 512), rg=_tile(t, ATT_ROW_GROUP))
    return pl.pallas_call(
        kern,
        grid=(b, A_HEADS, s_len // t),
        in_specs=[
            pl.BlockSpec(memory_space=pltpu.SMEM),
            pl.BlockSpec((1, t, hw), lambda bi, h, qi: (bi, qi, h)),
            pl.BlockSpec((1, s_len, hw), lambda bi, h, qi: (bi, 0, A_HEADS + h)),
            pl.BlockSpec((1, s_len, hw), lambda bi, h, qi: (bi, 0, 2 * A_HEADS + h)),
            pl.BlockSpec((1, 3, t, t), lambda bi, h, qi: (h, 0, 0, 0)),
            pl.BlockSpec((1, hw), lambda bi, h, qi: (0, 0)),
            pl.BlockSpec((1, hw), lambda bi, h, qi: (0, 0)),
            pl.BlockSpec((1, hw), lambda bi, h, qi: (0, 0)),
        ],
        out_specs=pl.BlockSpec((1, t, hw), lambda bi, h, qi: (bi, qi, h)),
        out_shape=jax.ShapeDtypeStruct((b, s_len, A_HEADS * hw), BF16),
        scratch_shapes=[
            pltpu.VMEM((s_len, hw), BF16),
            pltpu.VMEM((s_len, 2 * hw), BF16),
            pltpu.VMEM((2 * t, hw), F32),
            pltpu.VMEM((2 * t, 2 * hw), F32),
            pltpu.VMEM((2 * t, hw), BF16),
            pltpu.VMEM((2, 2 * t, t), F32),
        ],
        compiler_params=_params(("arbitrary", "arbitrary", "arbitrary"), 48),
        name="diff_attention",
    )(lam.reshape(1).astype(F32), qkv, qkv, qkv, nb, tile2(q_gain), tile2(k_gain),
      sub_gain.reshape(1, hw).astype(F32))


def _lru_kernel(xl_ref, gl_ref, cw_ref, cb_ref, wa_ref, ba_ref, wx_ref, bx_ref, c_ref, o_ref,
                xbuf, a_s, u_s, h_s, hc, *, ts, width):
    i = pl.program_id(1)
    halo = SUBLANES
    bd = width // LRU_BLOCKS

    @pl.when(i == 0)
    def _():
        xbuf[0:halo, :] = jnp.zeros((halo, width), F32)
        hc[...] = jnp.zeros(hc.shape, F32)

    x = xl_ref[0]
    xbuf[halo:halo + ts, :] = x
    xr = cb_ref[...] + cw_ref[0:1, :] * xbuf[pl.ds(halo - 3, ts), :]
    for j in range(1, CONV_WIDTH):
        xr = xr + cw_ref[j:j + 1, :] * xbuf[pl.ds(halo - 3 + j, ts), :]
    xbuf[0:halo, :] = x[ts - halo:, :]

    for g in range(LRU_BLOCKS):
        sl = slice(g * bd, (g + 1) * bd)
        xg = xr[:, sl]
        xb = xg.astype(BF16)
        r = jax.nn.sigmoid(jnp.dot(xb, wa_ref[g], preferred_element_type=F32) + ba_ref[:, sl])
        gi = jax.nn.sigmoid(jnp.dot(xb, wx_ref[g], preferred_element_type=F32) + bx_ref[:, sl])
        log_a = r * c_ref[:, sl]
        a = jnp.exp(log_a)
        a_s[:, sl] = a
        u_s[:, sl] = jnp.sqrt(-jnp.tanh(log_a) * (1.0 + a * a)) * (gi * xg)

    row = lax.broadcasted_iota(jnp.int32, (SUBLANES, width), 0)

    def body(gidx, h):
        r0 = pl.multiple_of(gidx * SUBLANES, SUBLANES)
        a = a_s[pl.ds(r0, SUBLANES), :]
        u = u_s[pl.ds(r0, SUBLANES), :]
        for d in (1, 2, 4):
            keep = row >= d
            u = jnp.where(keep, a * pltpu.roll(u, d, 0) + u, u)
            a = jnp.where(keep, a * pltpu.roll(a, d, 0), a)
        hr = a * h + u
        h_s[pl.ds(r0, SUBLANES), :] = hr
        return hr[SUBLANES - 1:SUBLANES, :]

    hc[...] = lax.fori_loop(0, ts // SUBLANES, body, hc[...], unroll=2)
    o_ref[0] = (h_s[...] * jax.nn.gelu(gl_ref[0])).astype(o_ref.dtype)


def _lru_branch(lru, conv_w, conv_b, wa, ba, wx, bx, lam_param):
    b, s_len, c2 = lru.shape
    width = c2 // 2
    ts = _tile(s_len, 256)
    bd = width // LRU_BLOCKS
    c_vec = (-LRU_C * jax.nn.softplus(-lam_param.astype(F32))).reshape(1, width)
    row = lambda v: v.reshape(1, width).astype(F32)
    const2 = lambda shape: pl.BlockSpec(shape, lambda bi, i: (0,) * len(shape))
    kern = functools.partial(_lru_kernel, ts=ts, width=width)
    return pl.pallas_call(
        kern,
        grid=(b, s_len // ts),
        in_specs=[
            pl.BlockSpec((1, ts, width), lambda bi, i: (bi, i, 0)),
            pl.BlockSpec((1, ts, width), lambda bi, i: (bi, i, 1)),
            const2((CONV_WIDTH, width)), const2((1, width)),
            const2((LRU_BLOCKS, bd, bd)), const2((1, width)),
            const2((LRU_BLOCKS, bd, bd)), const2((1, width)),
            const2((1, width)),
        ],
        out_specs=pl.BlockSpec((1, ts, width), lambda bi, i: (bi, i, 0)),
        out_shape=jax.ShapeDtypeStruct((b, s_len, width), BF16),
        scratch_shapes=[
            pltpu.VMEM((ts + SUBLANES, width), F32),
            pltpu.VMEM((ts, width), F32),
            pltpu.VMEM((ts, width), F32),
            pltpu.VMEM((ts, width), F32),
            pltpu.VMEM((1, width), F32),
        ],
        compiler_params=_params(("arbitrary", "arbitrary"), 40),
        name="rg_lru",
    )(lru, lru, conv_w.astype(F32), row(conv_b), wa.astype(BF16), row(ba), wx.astype(BF16), row(bx), c_vec)


def _merge_kernel(oa_ref, ob_ref, wa_ref, wb_ref, g0_ref, g1_ref, o_ref):
    ya = jnp.dot(oa_ref[...], wa_ref[...], preferred_element_type=F32)
    yb = jnp.dot(ob_ref[...], wb_ref[...], preferred_element_type=F32)
    o_ref[...] = (g0_ref[...].astype(F32) * ya + g1_ref[...].astype(F32) * yb).astype(o_ref.dtype)


def _merge(o_a, o_b, w_a, w_b, gates):
    m, ka = o_a.shape
    kb = o_b.shape[1]
    d = w_a.shape[1]
    tm, tn = _tile(m, 512), _tile(d, 1024)
    nj = d // tn
    return pl.pallas_call(
        _merge_kernel,
        grid=(m // tm, nj),
        in_specs=[
            pl.BlockSpec((tm, ka), lambda i, j: (i, 0)),
            pl.BlockSpec((tm, kb), lambda i, j: (i, 0)),
            pl.BlockSpec((ka, tn), lambda i, j: (0, j)),
            pl.BlockSpec((kb, tn), lambda i, j: (0, j)),
            pl.BlockSpec((tm, tn), lambda i, j: (i, j)),
            pl.BlockSpec((tm, tn), lambda i, j: (i, j + nj)),
        ],
        out_specs=pl.BlockSpec((tm, tn), lambda i, j: (i, j)),
        out_shape=jax.ShapeDtypeStruct((m, d), BF16),
        compiler_params=_params(("parallel", "parallel"), 40),
        name="branch_merge",
    )(o_a, o_b, w_a, w_b, gates, gates)


def _outproj_kernel(mx_ref, wo_ref, x_ref, g_ref, x1_ref, xn_ref, xn_hbm, nbuf, nsem, *, tm):
    i = pl.program_id(0)
    slot = i % 2
    _rows_out_reclaim(i, slot, nbuf, xn_hbm, nsem, tm)
    x1 = x_ref[...] + jnp.dot(mx_ref[...], wo_ref[...], preferred_element_type=F32)
    x1_ref[...] = x1
    xn = x1 * lax.rsqrt(jnp.mean(x1 * x1, axis=-1, keepdims=True) + RMS_EPS) * g_ref[...]
    xn_ref[...] = xn
    nbuf[slot] = xn
    _rows_out_send(i, pl.num_programs(0), slot, nbuf, xn_hbm, nsem, tm)


def _outproj(mixed, w_out, x, ln_ffn):
    m, d = x.shape
    tm = _tile(m, 512)
    row_blk = pl.BlockSpec((tm, d), lambda i: (i, 0))
    return pl.pallas_call(
        functools.partial(_outproj_kernel, tm=tm),
        grid=(m // tm,),
        in_specs=[row_blk, pl.BlockSpec((d, d), lambda i: (0, 0)), row_blk, pl.BlockSpec((1, d), lambda i: (0, 0))],
        out_specs=[row_blk, row_blk, pl.BlockSpec(memory_space=pl.ANY)],
        out_shape=[jax.ShapeDtypeStruct((m, d), F32), jax.ShapeDtypeStruct((m, d), F32),
                   jax.ShapeDtypeStruct((m, 1, d), F32)],
        scratch_shapes=[pltpu.VMEM((2, tm, d), F32), pltpu.SemaphoreType.DMA((2,))],
        compiler_params=_params(("arbitrary",), 60),
        name="outproj",
    )(mixed, w_out, x, ln_ffn.reshape(1, d).astype(F32))


def _router_kernel(xn_ref, w2_ref, wh_ref, br_ref, ti_ref, tg_ref, pos_ref, cnt_ref, carry, *, tm, sub):
    i = pl.program_id(0)

    @pl.when(i == 0)
    def _():
        carry[...] = jnp.zeros(carry.shape, F32)

    lane = lax.broadcasted_iota(jnp.int32, (sub, LANES), 1)
    rr = lax.broadcasted_iota(jnp.int32, (sub, sub), 0)
    cc = lax.broadcasted_iota(jnp.int32, (sub, sub), 1)
    below = (cc < rr).astype(BF16)

    for c in range(tm // sub):
        rs = slice(c * sub, (c + 1) * sub)
        xn = xn_ref[rs, :]
        xh = xn.astype(BF16)
        xl = (xn - xh.astype(F32)).astype(BF16)
        hi = jnp.dot(xh, w2_ref[...], preferred_element_type=F32)
        lo = jnp.dot(xl, wh_ref[...], preferred_element_type=F32)
        rest = hi[:, :LANES] + hi[:, LANES:] + lo + br_ref[...]

        vals, idxs = [], []
        for _ in range(TOP_K):
            mx = jnp.max(rest, axis=-1, keepdims=True)
            ix = jnp.min(jnp.where(rest == mx, lane, LANES), axis=-1, keepdims=True)
            vals.append(mx)
            idxs.append(ix)
            rest = jnp.where(lane == ix, NEG_SEL, rest)
        exps = [jnp.exp(v - vals[0]) for v in vals]
        den = exps[0]
        for e in exps[1:]:
            den = den + e
        ti = jnp.zeros((sub, LANES), jnp.int32)
        tg = jnp.zeros((sub, LANES), F32)
        sel = jnp.zeros((sub, LANES), F32)
        for k in range(TOP_K):
            ti = jnp.where(lane == k, idxs[k], ti)
            tg = jnp.where(lane == k, exps[k] / den, tg)
            sel = jnp.where(lane == idxs[k], 1.0, sel)
        ti_ref[rs, :] = ti
        tg_ref[rs, :] = tg

        pos = jnp.dot(below, sel.astype(BF16), preferred_element_type=F32) + carry[...]
        pos4 = jnp.zeros((sub, LANES), F32)
        for k in range(TOP_K):
            pk = jnp.sum(jnp.where(lane == idxs[k], pos, 0.0), axis=-1, keepdims=True)
            pos4 = jnp.where(lane == k, pk, pos4)
        pos_ref[rs, :] = pos4.astype(jnp.int32)
        carry[...] = carry[...] + jnp.sum(sel, axis=0, keepdims=True)
    cnt_ref[...] = carry[...].astype(jnp.int32)


def _router(xn, w_router, b_router):
    m, d = xn.shape
    e = w_router.shape[1]
    assert e <= LANES
    tm = _tile(m, 1024)
    wr = jnp.zeros((d, LANES), F32).at[:, :e].set(w_router.astype(F32))
    wh = wr.astype(BF16)
    wl = (wr - wh.astype(F32)).astype(BF16)
    br = jnp.full((1, LANES), NEG_BIG, F32).at[0, :e].set(b_router.astype(F32))
    row_blk = lambda w: pl.BlockSpec((tm, w), lambda i: (i, 0))
    const = lambda shape: pl.BlockSpec(shape, lambda i: (0, 0))
    return pl.pallas_call(
        functools.partial(_router_kernel, tm=tm, sub=_tile(tm, 256)),
        grid=(m // tm,),
        in_specs=[row_blk(d), const((d, 2 * LANES)), const((d, LANES)), const((1, LANES))],
        out_specs=[row_blk(LANES), row_blk(LANES), row_blk(LANES), const((1, LANES))],
        out_shape=[
            jax.ShapeDtypeStruct((m, LANES), jnp.int32),
            jax.ShapeDtypeStruct((m, LANES), F32),
            jax.ShapeDtypeStruct((m, LANES), jnp.int32),
            jax.ShapeDtypeStruct((1, LANES), jnp.int32),
        ],
        scratch_shapes=[pltpu.VMEM((1, LANES), F32)],
        compiler_params=_params(("arbitrary",), 40),
        name="router",
    )(xn, jnp.concatenate([wh, wl], axis=1), wh, br)


def _gather_kernel(nv_ref, idx_ref, nxt_ref, src_ref, *rest, rows, groups, weighted):
    if weighted:
        g_ref, base_ref, o_ref, buf, sem = rest
    else:
        o_ref, buf, sem = rest
    i = pl.program_id(0)
    n = nv_ref[0]
    total = rows * groups

    def row_copy(ref, r, slot):
        return pltpu.make_async_copy(src_ref.at[ref[0, 0, r]], buf.at[slot, pl.ds(r, 1), :], sem.at[slot])

    def issue(ref, slot):
        def body(r, carry):
            row_copy(ref, r, slot).start()
            return carry
        lax.fori_loop(0, total, body, 0, unroll=32)

    @pl.when(jnp.logical_and(i == 0, n > 0))
    def _():
        issue(idx_ref, 0)

    @pl.when(i + 1 < n)
    def _():
        issue(nxt_ref, (i + 1) % 2)

    slot = i % 2

    @pl.when(i < n)
    def _():
        def wait_body(r, carry):
            row_copy(idx_ref, r, slot).wait()
            return carry
        lax.fori_loop(0, total, wait_body, 0, unroll=8)

        if weighted:
            lane = lax.broadcasted_iota(jnp.int32, (rows, LANES), 1)
            g = g_ref[...]
            acc = base_ref[...]
            for k in range(groups):
                gk = jnp.sum(jnp.where(lane == k, g, 0.0), axis=-1, keepdims=True)
                acc = acc + gk * buf[slot, pl.ds(k * rows, rows), :]
            o_ref[...] = acc.astype(o_ref.dtype)
        else:
            o_ref[...] = buf[slot].astype(o_ref.dtype)

    @pl.when(i >= n)
    def _():
        o_ref[...] = jnp.zeros(o_ref.shape, o_ref.dtype)


def _gather_rows(src, idx, n_valid, rows, out_dtype, gates=None, base=None):
    steps, _, total = idx.shape
    groups = total // rows
    d = src.shape[2]
    weighted = gates is not None
    in_specs = [
        pl.BlockSpec(memory_space=pltpu.SMEM),
        pl.BlockSpec((1, 1, total), lambda i: (i, 0, 0), memory_space=pltpu.SMEM),
        pl.BlockSpec((1, 1, total), lambda i: (jnp.minimum(i + 1, steps - 1), 0, 0), memory_space=pltpu.SMEM),
        pl.BlockSpec(memory_space=pl.ANY),
    ]
    args = [n_valid, idx, idx, src]
    if weighted:
        in_specs += [pl.BlockSpec((rows, LANES), lambda i: (i, 0)), pl.BlockSpec((rows, d), lambda i: (i, 0))]
        args += [gates, base]
    return pl.pallas_call(
        functools.partial(_gather_kernel, rows=rows, groups=groups, weighted=weighted),
        grid=(steps,),
        in_specs=in_specs,
        out_specs=pl.BlockSpec((rows, d), lambda i: (i, 0)),
        out_shape=jax.ShapeDtypeStruct((steps * rows, d), out_dtype),
        scratch_shapes=[pltpu.VMEM((2, total, d), F32), pltpu.SemaphoreType.DMA((2,))],
        compiler_params=_params(("arbitrary",), 48),
        name="combine_rows" if weighted else "gather_rows",
    )(*args)


def _zero_unused_block(nu_ref, o_ref):
    @pl.when(pl.program_id(1) >= nu_ref[0])
    def _():
        o_ref[...] = jnp.zeros(o_ref.shape, o_ref.dtype)


def _moe1_kernel(be_ref, nu_ref, first_ref, x_ref, w_ref, bg_ref, bl_ref, o_ref, wp_ref, *, tn):
    i = pl.program_id(1)
    groups = tn // MXU_DIM
    half = MXU_DIM // 2
    _zero_unused_block(nu_ref, o_ref)

    @pl.when(first_ref[i] == 1)
    def _():
        rr = lax.broadcasted_iota(jnp.int32, (MXU_DIM, MXU_DIM), 0)
        cc = lax.broadcasted_iota(jnp.int32, (MXU_DIM, MXU_DIM), 1)
        src = jnp.where(cc < half, 2 * cc, 2 * (cc - half) + 1)
        perm = (rr == src).astype(BF16)
        for g in range(groups):
            sl = slice(g * MXU_DIM, (g + 1) * MXU_DIM)
            wt = w_ref[0, :, sl].astype(BF16)
            wp_ref[:, sl] = jnp.dot(wt, perm, preferred_element_type=F32).astype(BF16)

    @pl.when(i < nu_ref[0])
    def _():
        h = jnp.dot(x_ref[...], wp_ref[...], preferred_element_type=F32)
        for g in range(groups):
            fs = slice(g * half, (g + 1) * half)
            hg = h[:, g * MXU_DIM:g * MXU_DIM + half] + bg_ref[0, :, fs]
            hl = h[:, g * MXU_DIM + half:(g + 1) * MXU_DIM] + bl_ref[0, :, fs]
            glu = jnp.minimum(hg, SWIGLU_LIMIT)
            lin = jnp.clip(hl, -SWIGLU_LIMIT, SWIGLU_LIMIT)
            o_ref[:, fs] = (glu * jax.nn.sigmoid(SWIGLU_ALPHA * glu) * (lin + 1.0)).astype(o_ref.dtype)


def _moe2_kernel(be_ref, nu_ref, first_ref, a_ref, w_ref, b_ref, o_hbm, wb_ref, obuf, osem, *, bm):
    i = pl.program_id(1)
    slot = i % 2
    _rows_out_reclaim(i, slot, obuf, o_hbm, osem, bm)

    @pl.when(first_ref[i] == 1)
    def _():
        wb_ref[...] = w_ref[0].astype(BF16)

    @pl.when(i < nu_ref[0])
    def _():
        obuf[slot] = jnp.dot(a_ref[...], wb_ref[...], preferred_element_type=F32) + b_ref[0]

    @pl.when(i >= nu_ref[0])
    def _():
        obuf[slot] = jnp.zeros(obuf.shape[1:], obuf.dtype)

    _rows_out_send(i, pl.num_programs(1), slot, obuf, o_hbm, osem, bm)


def _grouped(kernel, name, rows_in, weight, biases, sched, bm, tn, out_cols_per_tile, out_dtype, token_rows_out=False):
    p, k = rows_in.shape
    n = weight.shape[2]
    nblk = p // bm
    tb = out_cols_per_tile
    blk = lambda i, nu: jnp.maximum(jnp.minimum(i, nu[0] - 1), 0)
    scratch = [pltpu.VMEM((k, tn), BF16)]
    if token_rows_out:
        assert n == tn == tb
        out_spec = pl.BlockSpec(memory_space=pl.ANY)
        out_shape = jax.ShapeDtypeStruct((p, 1, n), out_dtype)
        scratch += [pltpu.VMEM((2, bm, n), out_dtype), pltpu.SemaphoreType.DMA((2,))]
    else:
        out_spec = pl.BlockSpec((bm, tb), lambda j, i, be, nu, fi: (i, j))
        out_shape = jax.ShapeDtypeStruct((p, (n // tn) * tb), out_dtype)
    grid_spec = pltpu.PrefetchScalarGridSpec(
        num_scalar_prefetch=3,
        grid=(n // tn, nblk),
        in_specs=[pl.BlockSpec((bm, k), lambda j, i, be, nu, fi: (blk(i, nu), 0)),
                  pl.BlockSpec((1, k, tn), lambda j, i, be, nu, fi: (be[i], 0, j))]
        + [pl.BlockSpec((1, 1, tb), lambda j, i, be, nu, fi: (be[i], 0, j))] * len(biases),
        out_specs=out_spec,
        scratch_shapes=scratch,
    )
    return pl.pallas_call(
        kernel,
        grid_spec=grid_spec,
        out_shape=out_shape,
        compiler_params=_params(("arbitrary", "arbitrary"), 60),
        name=name,
    )(*sched, rows_in, weight, *biases)


def _moe(x1, xn, ti, tg, pos, cnt, w1, b1, w2, b2):
    t_tok, d = x1.shape
    n_exp, _, f2 = w1.shape
    f = f2 // 2
    bm = MOE_ROWS
    assert (t_tok * TOP_K) % bm == 0
    nblk = t_tok * TOP_K // bm + n_exp
    p = nblk * bm

    ti4 = ti[:, :TOP_K]
    counts = cnt[0, :n_exp]
    padded = (counts + bm - 1) // bm * bm
    pad_ends = jnp.cumsum(padded)
    pad_starts = pad_ends - padded
    onehot = ti4[:, :, None] == jnp.arange(n_exp, dtype=jnp.int32)[None, None, :]
    dest4 = jnp.sum(jnp.where(onehot, pad_starts[None, None, :], 0), axis=-1) + pos[:, :TOP_K]
    n_used = (pad_ends[-1] // bm).astype(jnp.int32).reshape(1)
    blk_start = jnp.arange(nblk, dtype=jnp.int32) * bm
    blk_start = jnp.minimum(blk_start, pad_ends[-1] - bm)
    blk_expert = jnp.sum(blk_start[:, None] >= pad_ends[None, :], axis=1).astype(jnp.int32)
    blk_expert = jnp.minimum(blk_expert, n_exp - 1)
    first = jnp.concatenate([jnp.ones((1,), jnp.int32), (blk_expert[1:] != blk_expert[:-1]).astype(jnp.int32)])
    sched = (blk_expert, n_used, first)
    tok = jnp.broadcast_to(jnp.arange(t_tok, dtype=jnp.int32)[:, None], (t_tok, TOP_K))
    row_tok = (jnp.arange(p, dtype=jnp.int32) % t_tok).at[dest4.reshape(-1)].set(tok.reshape(-1), unique_indices=True)

    gr = _tile(bm, GATHER_ROWS)
    assert bm % gr == 0
    xg = _gather_rows(xn, row_tok.reshape(p // gr, 1, gr), n_used * (bm // gr), gr, BF16)

    b1g = b1[:, 0::2].reshape(n_exp, 1, f).astype(F32)
    b1l = b1[:, 1::2].reshape(n_exp, 1, f).astype(F32)
    tn1 = _tile(f2, 2048)
    assert tn1 % MXU_DIM == 0
    act = _grouped(functools.partial(_moe1_kernel, tn=tn1), "expert_up", xg, w1, [b1g, b1l], sched,
                   bm, tn1, tn1 // 2, BF16)
    out = _grouped(functools.partial(_moe2_kernel, bm=bm), "expert_down", act, w2,
                   [b2.reshape(n_exp, 1, d).astype(F32)], sched, bm, d, d, F32, token_rows_out=True)

    tc = _tile(t_tok, COMBINE_TOKENS)
    steps = t_tok // tc
    idx = jnp.transpose(dest4.reshape(steps, tc, TOP_K), (0, 2, 1)).reshape(steps, 1, TOP_K * tc)
    return _gather_rows(out, idx, jnp.full((1,), steps, jnp.int32), tc, F32, gates=tg, base=x1)


def kernel(x, ln_mix, w_in, b_gate, q_gain, k_gain, lambda_q1, lambda_k1, lambda_q2, lambda_k2, sub_gain, rel_table, conv_w, conv_b, lru_wa, lru_ba, lru_wx, lru_bx, lru_lambda, w_branch_a, w_branch_b, w_out, ln_ffn, w_router, b_router, w1, b1, w2, b2):
    b, s_len, d = x.shape
    t_tok = b * s_len
    depth = ln_mix.shape[0]
    assert depth == 1
    q_cols = A_HEADS * 2 * A_QK_DIM
    qkv_cols = 2 * q_cols + A_HEADS * A_V_DIM
    lru_w = d // 2
    xt = x.reshape(t_tok, d)
    l = 0

    hn = _rmsnorm(xt, ln_mix[l], BF16)
    w_in_b = w_in[l].astype(BF16)
    qkv = _proj(hn, w_in_b[:, :qkv_cols], BF16)
    lru = _proj(hn, w_in_b[:, qkv_cols:qkv_cols + 2 * lru_w], F32)
    gates = _proj(hn, w_in_b[:, qkv_cols + 2 * lru_w:], BF16, bias=b_gate[l])

    lam = (jnp.exp(jnp.sum(lambda_q1[l].astype(F32) * lambda_k1[l].astype(F32)))
           - jnp.exp(jnp.sum(lambda_q2[l].astype(F32) * lambda_k2[l].astype(F32))) + LAM_INIT)
    o_a = _diff_attention(qkv.reshape(b, s_len, qkv_cols), rel_table, lam, q_gain[l], k_gain[l], sub_gain[l])
    o_b = _lru_branch(lru.reshape(b, s_len, 2 * lru_w), conv_w[l], conv_b[l], lru_wa[l], lru_ba[l],
                      lru_wx[l], lru_bx[l], lru_lambda[l])

    mixed = _merge(o_a.reshape(t_tok, -1), o_b.reshape(t_tok, lru_w),
                   w_branch_a[l].astype(BF16), w_branch_b[l].astype(BF16), gates)
    x1, xn, xn_rows = _outproj(mixed, w_out[l].astype(BF16), xt, ln_ffn[l])
    ti, tg, pos, cnt = _router(xn, w_router[l], b_router[l])
    y = _moe(x1, xn_rows, ti, tg, pos, cnt, w1[l], b1[l], w2[l], b2[l])
    return y.reshape(b, s_len, d)
```

```python
import functools
import math

import jax
import jax.numpy as jnp
from jax import lax
from jax.experimental import pallas as pl
from jax.experimental.pallas import tpu as pltpu

F32 = jnp.float32
BF16 = jnp.bfloat16

CHUNK = 64
RMS_EPS = 1e-6
A_HEADS = 8
A_QK_DIM = 64
A_V_DIM = 2 * A_QK_DIM
LRU_BLOCKS = 8
CONV_WIDTH = 4
LRU_C = 8.0
REL_BUCKETS = 32
REL_MAX_DIST = 128
TOP_K = 4
SWIGLU_LIMIT = 7.0
SWIGLU_ALPHA = 1.702
LAM_INIT = 0.8 - 0.6 * math.exp(-0.3 * 0)
LOG2E = 1.4426950408889634

LANES = 128
SUBLANES = 8
MXU_DIM = 256
NEG_BIG = -1e30
NEG_SEL = -3e38

ATT_Q_BLOCK = 512
ATT_ROW_GROUP = 128
MOE_ROWS = 512
GATHER_ROWS = 512
COMBINE_TOKENS = 128


def _tile(n, pref):
    t = min(n, pref)
    assert n % t == 0, (n, pref)
    return t


def _params(semantics, vmem_mib):
    return pltpu.CompilerParams(dimension_semantics=semantics, vmem_limit_bytes=vmem_mib * 1024 * 1024)


def _rows_out_copy(step, slot, buf, o_hbm, sem, rows):
    return pltpu.make_async_copy(buf.at[slot], o_hbm.at[pl.ds(step * rows, rows), 0, :], sem.at[slot])


def _rows_out_reclaim(i, slot, buf, o_hbm, sem, rows):
    @pl.when(i >= 2)
    def _():
        _rows_out_copy(i - 2, slot, buf, o_hbm, sem, rows).wait()


def _rows_out_send(i, n, slot, buf, o_hbm, sem, rows):
    _rows_out_copy(i, slot, buf, o_hbm, sem, rows).start()

    @pl.when(i == n - 1)
    def _():
        @pl.when(n > 1)
        def _():
            _rows_out_copy(i - 1, 1 - slot, buf, o_hbm, sem, rows).wait()
        _rows_out_copy(i, slot, buf, o_hbm, sem, rows).wait()


def _rmsnorm_kernel(x_ref, g_ref, o_ref):
    x = x_ref[...]
    y = x * lax.rsqrt(jnp.mean(x * x, axis=-1, keepdims=True) + RMS_EPS)
    o_ref[...] = (y * g_ref[...]).astype(o_ref.dtype)


def _rmsnorm(x, g, out_dtype):
    m, d = x.shape
    tm = _tile(m, 512)
    return pl.pallas_call(
        _rmsnorm_kernel,
        grid=(m // tm,),
        in_specs=[pl.BlockSpec((tm, d), lambda i: (i, 0)), pl.BlockSpec((1, d), lambda i: (0, 0))],
        out_specs=pl.BlockSpec((tm, d), lambda i: (i, 0)),
        out_shape=jax.ShapeDtypeStruct((m, d), out_dtype),
        compiler_params=_params(("parallel",), 32),
        name="rmsnorm",
    )(x, g.reshape(1, d))


def _proj_kernel(a_ref, w_ref, *rest, sigmoid_bias):
    if sigmoid_bias:
        b_ref, o_ref = rest
    else:
        (o_ref,) = rest
    acc = jnp.dot(a_ref[...], w_ref[...], preferred_element_type=F32)
    if sigmoid_bias:
        acc = jax.nn.sigmoid(acc + b_ref[...])
    o_ref[...] = acc.astype(o_ref.dtype)


def _proj(a, w, col0, n, out_dtype, bias=None, tm=1024, tn=1024):
    m, k = a.shape
    tm, tn = _tile(m, tm), math.gcd(_tile(n, tn), col0)
    assert n % tn == 0 and tn % LANES == 0
    j0 = col0 // tn
    in_specs = [pl.BlockSpec((tm, k), lambda i, j: (i, 0)), pl.BlockSpec((k, tn), lambda i, j: (0, j + j0))]
    args = [a, w]
    if bias is not None:
        in_specs.append(pl.BlockSpec((1, tn), lambda i, j: (0, j)))
        args.append(bias.reshape(1, n))
    return pl.pallas_call(
        functools.partial(_proj_kernel, sigmoid_bias=bias is not None),
        grid=(m // tm, n // tn),
        in_specs=in_specs,
        out_specs=pl.BlockSpec((tm, tn), lambda i, j: (i, j)),
        out_shape=jax.ShapeDtypeStruct((m, n), out_dtype),
        compiler_params=_params(("parallel", "parallel"), 48),
        name="proj",
    )(*args)


def _t5_bucket(rel):
    nb = REL_BUCKETS // 2
    max_exact = nb // 2
    ret = jnp.where(rel > 0, nb, 0)
    n = jnp.abs(rel)
    nf = jnp.maximum(n, 1).astype(F32)
    large = max_exact + (jnp.log(nf / max_exact) / math.log(REL_MAX_DIST / max_exact)
                         * (nb - max_exact)).astype(jnp.int32)
    large = jnp.minimum(large, nb - 1)
    return ret + jnp.where(n < max_exact, n, large)


def _near_bias(rel_table, t):
    assert t + 1 >= REL_MAX_DIST
    table = rel_table.astype(F32)
    r = jnp.arange(t, dtype=jnp.int32)[:, None]
    c = jnp.arange(t, dtype=jnp.int32)[None, :]

    def lookup(bucket):
        out = jnp.zeros((A_HEADS,) + bucket.shape, F32)
        for bkt in range(REL_BUCKETS):
            out = jnp.where(bucket[None] == bkt, table[bkt][:, None, None], out)
        return out

    far = table[_t5_bucket(jnp.int32(-(t + 1)))][:, None, None]
    diag = lookup(_t5_bucket(c - r)) - far
    allowed = (c // CHUNK) <= (r // CHUNK)
    diag = jnp.where(allowed[None], diag * LOG2E, NEG_BIG)
    sub = (lookup(_t5_bucket(c - r - t)) - far) * LOG2E
    return jnp.stack([jnp.zeros_like(diag), sub, diag], axis=1)


def _attn_kernel(lam_ref, qa_ref, qb_ref, k_ref, v_ref, nb_ref, qg_ref, kg_ref, sg_ref, oa_ref, ob_ref,
                 kn_ref, vx_ref, m_ref, acc_ref, qs_ref, s_ref, *, t, nq, s_len, k_chunk, rg):
    g = pl.program_id(2)
    blk_a = g
    blk_b = nq - 1 - g
    hw = 2 * A_QK_DIM
    rows = 2 * t
    lo = lax.broadcasted_iota(jnp.int32, (1, hw), 1) < A_QK_DIM

    def qk_norm(x, gain):
        sq = x * x
        s_lo = jnp.sum(jnp.where(lo, sq, 0.0), axis=-1, keepdims=True)
        s_hi = jnp.sum(jnp.where(lo, 0.0, sq), axis=-1, keepdims=True)
        ms = jnp.where(lo, s_lo, s_hi) * (1.0 / A_QK_DIM)
        return x * lax.rsqrt(ms + RMS_EPS) * gain

    @pl.when(g == 0)
    def _():
        def body(c, carry):
            r0 = pl.multiple_of(c * k_chunk, k_chunk)
            kk = k_ref[0, pl.ds(r0, k_chunk), :].astype(F32)
            kn_ref[pl.ds(r0, k_chunk), :] = qk_norm(kk, kg_ref[...]).astype(BF16)
            vx_ref[pl.ds(r0, k_chunk), 0:hw] = v_ref[0, pl.ds(r0, k_chunk), :]
            vx_ref[pl.ds(r0, k_chunk), hw:2 * hw] = jnp.ones((k_chunk, hw), BF16)
            return carry
        lax.fori_loop(0, s_len // k_chunk, body, 0)

    for which, q_ref in ((0, qa_ref), (1, qb_ref)):
        q = qk_norm(q_ref[0].astype(F32), qg_ref[...]) * (A_QK_DIM ** -0.5 * LOG2E)
        qs_ref[which] = jnp.concatenate([jnp.where(lo, q, 0.0), jnp.where(lo, 0.0, q)], axis=0).astype(BF16)

    m_ref[...] = jnp.full(m_ref.shape, NEG_BIG, F32)
    acc_ref[...] = jnp.zeros(acc_ref.shape, F32)

    groups = [slice(c * rg, (c + 1) * rg) for c in range(rows // rg)]

    def item(i):
        which = (i > blk_a).astype(jnp.int32)
        qblk = jnp.where(i > blk_a, blk_b, blk_a)
        return which, qblk, i - which * (blk_a + 1)

    def logits(i, slot):
        which, qblk, kb = item(i)
        kj = kn_ref[pl.ds(pl.multiple_of(kb * t, t), t), :]
        tile = jnp.clip(kb - (qblk - 2), 0, 2)
        for rs in groups:
            s = lax.dot_general(qs_ref[which, rs, :], kj, (((1,), (1,)), ((), ())), preferred_element_type=F32)
            b0 = rs.start % t
            s_ref[slot, rs, :] = s + nb_ref[0, tile, b0:b0 + rg, :]

    def accumulate(i, slot):
        which, _, kb = item(i)
        vj = vx_ref[pl.ds(pl.multiple_of(kb * t, t), t), :]
        for rs in groups:
            s = s_ref[slot, rs, :]
            chunks = [s[:, c * LANES:(c + 1) * LANES] for c in range(t // LANES)]
            mc = chunks[0]
            for ch in chunks[1:]:
                mc = jnp.maximum(mc, ch)
            m_prev = m_ref[which, rs, :]
            m_new = jnp.maximum(m_prev, jnp.max(mc, axis=-1, keepdims=True))
            alpha = jnp.exp2(m_prev - m_new)
            p = jnp.concatenate([jnp.exp2(ch - m_new) for ch in chunks], axis=1).astype(BF16)
            pv = jnp.dot(p, vj, preferred_element_type=F32)
            acc_ref[which, rs, :] = jnp.concatenate([alpha, alpha], axis=1) * acc_ref[which, rs, :] + pv
            m_ref[which, rs, :] = m_new

    logits(0, 0)
    for i in range(nq + 1):
        accumulate(i, i % 2)
        if i < nq:
            logits(i + 1, (i + 1) % 2)

    lam = lam_ref[0]
    for which, o_ref in ((0, oa_ref), (1, ob_ref)):
        acc = acc_ref[which]
        o = acc[:t, :hw] / acc[:t, hw:] - lam * (acc[t:, :hw] / acc[t:, hw:])
        o = o * lax.rsqrt(jnp.mean(o * o, axis=-1, keepdims=True) + RMS_EPS) * sg_ref[...]
        o_ref[0] = (o * (1.0 - LAM_INIT)).astype(o_ref.dtype)


def _diff_attention(qkv, rel_table, lam, q_gain, k_gain, sub_gain):
    b, s_len, _ = qkv.shape
    t = _tile(s_len, ATT_Q_BLOCK)
    nq = s_len // t
    assert nq % 2 == 0
    half = nq // 2
    hw = 2 * A_QK_DIM
    nb = _near_bias(rel_table, t)
    tile2 = lambda gain: jnp.concatenate([gain, gain]).reshape(1, hw).astype(F32)
    kern = functools.partial(_attn_kernel, t=t, nq=nq, s_len=s_len, k_chunk=_tile(s_len, 512),
                             rg=_tile(t, ATT_ROW_GROUP))
    const = pl.BlockSpec((1, hw), lambda bi, h, g: (0, 0))
    o_lo, o_hi = pl.pallas_call(
        kern,
        grid=(b, A_HEADS, half),
        in_specs=[
            pl.BlockSpec(memory_space=pltpu.SMEM),
            pl.BlockSpec((1, t, hw), lambda bi, h, g: (bi, g, h)),
            pl.BlockSpec((1, t, hw), lambda bi, h, g: (bi, nq - 1 - g, h)),
            pl.BlockSpec((1, s_len, hw), lambda bi, h, g: (bi, 0, A_HEADS + h)),
            pl.BlockSpec((1, s_len, hw), lambda bi, h, g: (bi, 0, 2 * A_HEADS + h)),
            pl.BlockSpec((1, 3, t, t), lambda bi, h, g: (h, 0, 0, 0)),
            const, const, const,
        ],
        out_specs=[pl.BlockSpec((1, t, hw), lambda bi, h, g: (bi, g, h)),
                   pl.BlockSpec((1, t, hw), lambda bi, h, g: (bi, half - 1 - g, h))],
        out_shape=[jax.ShapeDtypeStruct((b, s_len // 2, A_HEADS * hw), BF16)] * 2,
        scratch_shapes=[
            pltpu.VMEM((s_len, hw), BF16),
            pltpu.VMEM((s_len, 2 * hw), BF16),
            pltpu.VMEM((2, 2 * t, hw), F32),
            pltpu.VMEM((2, 2 * t, 2 * hw), F32),
            pltpu.VMEM((2, 2 * t, hw), BF16),
            pltpu.VMEM((2, 2 * t, t), F32),
        ],
        compiler_params=_params(("arbitrary", "arbitrary", "arbitrary"), 48),
        name="diff_attention",
    )(lam.reshape(1).astype(F32), qkv, qkv, qkv, qkv, nb, tile2(q_gain), tile2(k_gain),
      sub_gain.reshape(1, hw).astype(F32))
    return jnp.concatenate([o_lo, o_hi], axis=1)


def _lru_kernel(xl_ref, gl_ref, cw_ref, cb_ref, wa_ref, ba_ref, wx_ref, bx_ref, c_ref, o_ref,
                xbuf, a_s, u_s, h_s, hc, *, ts, width):
    i = pl.program_id(1)
    halo = SUBLANES
    bd = width // LRU_BLOCKS

    @pl.when(i == 0)
    def _():
        xbuf[0:halo, :] = jnp.zeros((halo, width), F32)
        hc[...] = jnp.zeros(hc.shape, F32)

    x = xl_ref[0]
    xbuf[halo:halo + ts, :] = x
    xr = cb_ref[...] + cw_ref[0:1, :] * xbuf[pl.ds(halo - 3, ts), :]
    for j in range(1, CONV_WIDTH):
        xr = xr + cw_ref[j:j + 1, :] * xbuf[pl.ds(halo - 3 + j, ts), :]
    xbuf[0:halo, :] = x[ts - halo:, :]

    for g in range(LRU_BLOCKS):
        sl = slice(g * bd, (g + 1) * bd)
        xg = xr[:, sl]
        xb = xg.astype(BF16)
        r = jax.nn.sigmoid(jnp.dot(xb, wa_ref[g], preferred_element_type=F32) + ba_ref[:, sl])
        gi = jax.nn.sigmoid(jnp.dot(xb, wx_ref[g], preferred_element_type=F32) + bx_ref[:, sl])
        log_a = r * c_ref[:, sl]
        a = jnp.exp(log_a)
        a_s[:, sl] = a
        u_s[:, sl] = jnp.sqrt(-jnp.tanh(log_a) * (1.0 + a * a)) * (gi * xg)

    row = lax.broadcasted_iota(jnp.int32, (SUBLANES, width), 0)

    def body(gidx, h):
        r0 = pl.multiple_of(gidx * SUBLANES, SUBLANES)
        a = a_s[pl.ds(r0, SUBLANES), :]
        u = u_s[pl.ds(r0, SUBLANES), :]
        for d in (1, 2, 4):
            keep = row >= d
            u = jnp.where(keep, a * pltpu.roll(u, d, 0) + u, u)
            a = jnp.where(keep, a * pltpu.roll(a, d, 0), a)
        hr = a * h + u
        h_s[pl.ds(r0, SUBLANES), :] = hr
        return hr[SUBLANES - 1:SUBLANES, :]

    hc[...] = lax.fori_loop(0, ts // SUBLANES, body, hc[...], unroll=2)
    o_ref[0] = (h_s[...] * jax.nn.gelu(gl_ref[0])).astype(o_ref.dtype)


def _lru_branch(lru, conv_w, conv_b, wa, ba, wx, bx, lam_param):
    b, s_len, c2 = lru.shape
    width = c2 // 2
    ts = _tile(s_len, 256)
    bd = width // LRU_BLOCKS
    c_vec = (-LRU_C * jax.nn.softplus(-lam_param.astype(F32))).reshape(1, width)
    row = lambda v: v.reshape(1, width).astype(F32)
    const2 = lambda shape: pl.BlockSpec(shape, lambda bi, i: (0,) * len(shape))
    kern = functools.partial(_lru_kernel, ts=ts, width=width)
    return pl.pallas_call(
        kern,
        grid=(b, s_len // ts),
        in_specs=[
            pl.BlockSpec((1, ts, width), lambda bi, i: (bi, i, 0)),
            pl.BlockSpec((1, ts, width), lambda bi, i: (bi, i, 1)),
            const2((CONV_WIDTH, width)), const2((1, width)),
            const2((LRU_BLOCKS, bd, bd)), const2((1, width)),
            const2((LRU_BLOCKS, bd, bd)), const2((1, width)),
            const2((1, width)),
        ],
        out_specs=pl.BlockSpec((1, ts, width), lambda bi, i: (bi, i, 0)),
        out_shape=jax.ShapeDtypeStruct((b, s_len, width), BF16),
        scratch_shapes=[
            pltpu.VMEM((ts + SUBLANES, width), F32),
            pltpu.VMEM((ts, width), F32),
            pltpu.VMEM((ts, width), F32),
            pltpu.VMEM((ts, width), F32),
            pltpu.VMEM((1, width), F32),
        ],
        compiler_params=_params(("arbitrary", "arbitrary"), 40),
        name="rg_lru",
    )(lru, lru, conv_w.astype(F32), row(conv_b), wa.astype(BF16), row(ba), wx.astype(BF16), row(bx), c_vec)


def _merge_kernel(oa_ref, ob_ref, wa_ref, wb_ref, g0_ref, g1_ref, o_ref):
    ya = jnp.dot(oa_ref[...], wa_ref[...], preferred_element_type=F32)
    yb = jnp.dot(ob_ref[...], wb_ref[...], preferred_element_type=F32)
    o_ref[...] = (g0_ref[...].astype(F32) * ya + g1_ref[...].astype(F32) * yb).astype(o_ref.dtype)


def _merge(o_a, o_b, w_a, w_b, gates):
    m, ka = o_a.shape
    kb = o_b.shape[1]
    d = w_a.shape[1]
    tm, tn = _tile(m, 512), _tile(d, 1024)
    nj = d // tn
    return pl.pallas_call(
        _merge_kernel,
        grid=(m // tm, nj),
        in_specs=[
            pl.BlockSpec((tm, ka), lambda i, j: (i, 0)),
            pl.BlockSpec((tm, kb), lambda i, j: (i, 0)),
            pl.BlockSpec((ka, tn), lambda i, j: (0, j)),
            pl.BlockSpec((kb, tn), lambda i, j: (0, j)),
            pl.BlockSpec((tm, tn), lambda i, j: (i, j)),
            pl.BlockSpec((tm, tn), lambda i, j: (i, j + nj)),
        ],
        out_specs=pl.BlockSpec((tm, tn), lambda i, j: (i, j)),
        out_shape=jax.ShapeDtypeStruct((m, d), BF16),
        compiler_params=_params(("parallel", "parallel"), 40),
        name="branch_merge",
    )(o_a, o_b, w_a, w_b, gates, gates)


def _outproj_kernel(mx_ref, wo_ref, x_ref, g_ref, x1_ref, xn_ref, xn_hbm, nbuf, nsem, *, tm):
    i = pl.program_id(0)
    slot = i % 2
    _rows_out_reclaim(i, slot, nbuf, xn_hbm, nsem, tm)
    x1 = x_ref[...] + jnp.dot(mx_ref[...], wo_ref[...], preferred_element_type=F32)
    x1_ref[...] = x1
    xn = x1 * lax.rsqrt(jnp.mean(x1 * x1, axis=-1, keepdims=True) + RMS_EPS) * g_ref[...]
    xn_ref[...] = xn
    nbuf[slot] = xn
    _rows_out_send(i, pl.num_programs(0), slot, nbuf, xn_hbm, nsem, tm)


def _outproj(mixed, w_out, x, ln_ffn):
    m, d = x.shape
    tm = _tile(m, 512)
    row_blk = pl.BlockSpec((tm, d), lambda i: (i, 0))
    return pl.pallas_call(
        functools.partial(_outproj_kernel, tm=tm),
        grid=(m // tm,),
        in_specs=[row_blk, pl.BlockSpec((d, d), lambda i: (0, 0)), row_blk, pl.BlockSpec((1, d), lambda i: (0, 0))],
        out_specs=[row_blk, row_blk, pl.BlockSpec(memory_space=pl.ANY)],
        out_shape=[jax.ShapeDtypeStruct((m, d), F32), jax.ShapeDtypeStruct((m, d), F32),
                   jax.ShapeDtypeStruct((m, 1, d), F32)],
        scratch_shapes=[pltpu.VMEM((2, tm, d), F32), pltpu.SemaphoreType.DMA((2,))],
        compiler_params=_params(("arbitrary",), 60),
        name="outproj",
    )(mixed, w_out, x, ln_ffn.reshape(1, d).astype(F32))


def _router_kernel(xn_ref, w2_ref, wh_ref, br_ref, ti_ref, tg_ref, pos_ref, cnt_ref, carry, *, tm, sub):
    i = pl.program_id(0)

    @pl.when(i == 0)
    def _():
        carry[...] = jnp.zeros(carry.shape, F32)

    lane = lax.broadcasted_iota(jnp.int32, (sub, LANES), 1)
    rr = lax.broadcasted_iota(jnp.int32, (sub, sub), 0)
    cc = lax.broadcasted_iota(jnp.int32, (sub, sub), 1)
    below = (cc < rr).astype(BF16)

    for c in range(tm // sub):
        rs = slice(c * sub, (c + 1) * sub)
        xn = xn_ref[rs, :]
        xh = xn.astype(BF16)
        xl = (xn - xh.astype(F32)).astype(BF16)
        hi = jnp.dot(xh, w2_ref[...], preferred_element_type=F32)
        lo = jnp.dot(xl, wh_ref[...], preferred_element_type=F32)
        rest = hi[:, :LANES] + hi[:, LANES:] + lo + br_ref[...]

        vals, idxs = [], []
        for _ in range(TOP_K):
            mx = jnp.max(rest, axis=-1, keepdims=True)
            ix = jnp.min(jnp.where(rest == mx, lane, LANES), axis=-1, keepdims=True)
            vals.append(mx)
            idxs.append(ix)
            rest = jnp.where(lane == ix, NEG_SEL, rest)
        exps = [jnp.exp(v - vals[0]) for v in vals]
        den = exps[0]
        for e in exps[1:]:
            den = den + e
        ti = jnp.zeros((sub, LANES), jnp.int32)
        tg = jnp.zeros((sub, LANES), F32)
        sel = jnp.zeros((sub, LANES), F32)
        for k in range(TOP_K):
            ti = jnp.where(lane == k, idxs[k], ti)
            tg = jnp.where(lane == k, exps[k] / den, tg)
            sel = jnp.where(lane == idxs[k], 1.0, sel)
        ti_ref[rs, :] = ti
        tg_ref[rs, :] = tg

        pos = jnp.dot(below, sel.astype(BF16), preferred_element_type=F32) + carry[...]
        pos4 = jnp.zeros((sub, LANES), F32)
        for k in range(TOP_K):
            pk = jnp.sum(jnp.where(lane == idxs[k], pos, 0.0), axis=-1, keepdims=True)
            pos4 = jnp.where(lane == k, pk, pos4)
        pos_ref[rs, :] = pos4.astype(jnp.int32)
        carry[...] = carry[...] + jnp.sum(sel, axis=0, keepdims=True)
    cnt_ref[...] = carry[...].astype(jnp.int32)


def _router(xn, w_router, b_router):
    m, d = xn.shape
    e = w_router.shape[1]
    assert e <= LANES
    tm = _tile(m, 1024)
    wr = jnp.zeros((d, LANES), F32).at[:, :e].set(w_router.astype(F32))
    wh = wr.astype(BF16)
    wl = (wr - wh.astype(F32)).astype(BF16)
    br = jnp.full((1, LANES), NEG_BIG, F32).at[0, :e].set(b_router.astype(F32))
    row_blk = lambda w: pl.BlockSpec((tm, w), lambda i: (i, 0))
    const = lambda shape: pl.BlockSpec(shape, lambda i: (0, 0))
    return pl.pallas_call(
        functools.partial(_router_kernel, tm=tm, sub=_tile(tm, 256)),
        grid=(m // tm,),
        in_specs=[row_blk(d), const((d, 2 * LANES)), const((d, LANES)), const((1, LANES))],
        out_specs=[row_blk(LANES), row_blk(LANES), row_blk(LANES), const((1, LANES))],
        out_shape=[
            jax.ShapeDtypeStruct((m, LANES), jnp.int32),
            jax.ShapeDtypeStruct((m, LANES), F32),
            jax.ShapeDtypeStruct((m, LANES), jnp.int32),
            jax.ShapeDtypeStruct((1, LANES), jnp.int32),
        ],
        scratch_shapes=[pltpu.VMEM((1, LANES), F32)],
        compiler_params=_params(("arbitrary",), 40),
        name="router",
    )(xn, jnp.concatenate([wh, wl], axis=1), wh, br)


def _gather_kernel(nv_ref, idx_ref, nxt_ref, src_ref, *rest, rows, groups, weighted):
    if weighted:
        g_ref, base_ref, o_ref, buf, sem = rest
    else:
        o_ref, buf, sem = rest
    i = pl.program_id(0)
    n = nv_ref[0]
    total = rows * groups

    def row_copy(ref, r, slot):
        return pltpu.make_async_copy(src_ref.at[ref[0, 0, r]], buf.at[slot, pl.ds(r, 1), :], sem.at[slot])

    def issue(ref, slot):
        def body(r, carry):
            row_copy(ref, r, slot).start()
            return carry
        lax.fori_loop(0, total, body, 0, unroll=32)

    @pl.when(jnp.logical_and(i == 0, n > 0))
    def _():
        issue(idx_ref, 0)

    @pl.when(i + 1 < n)
    def _():
        issue(nxt_ref, (i + 1) % 2)

    slot = i % 2

    @pl.when(i < n)
    def _():
        def wait_body(r, carry):
            row_copy(idx_ref, r, slot).wait()
            return carry
        lax.fori_loop(0, total, wait_body, 0, unroll=8)

        if weighted:
            lane = lax.broadcasted_iota(jnp.int32, (rows, LANES), 1)
            g = g_ref[...]
            acc = base_ref[...]
            for k in range(groups):
                gk = jnp.sum(jnp.where(lane == k, g, 0.0), axis=-1, keepdims=True)
                acc = acc + gk * buf[slot, pl.ds(k * rows, rows), :]
            o_ref[...] = acc.astype(o_ref.dtype)
        else:
            o_ref[...] = buf[slot].astype(o_ref.dtype)

    @pl.when(i >= n)
    def _():
        o_ref[...] = jnp.zeros(o_ref.shape, o_ref.dtype)


def _gather_rows(src, idx, n_valid, rows, out_dtype, gates=None, base=None):
    steps, _, total = idx.shape
    groups = total // rows
    d = src.shape[2]
    weighted = gates is not None
    in_specs = [
        pl.BlockSpec(memory_space=pltpu.SMEM),
        pl.BlockSpec((1, 1, total), lambda i: (i, 0, 0), memory_space=pltpu.SMEM),
        pl.BlockSpec((1, 1, total), lambda i: (jnp.minimum(i + 1, steps - 1), 0, 0), memory_space=pltpu.SMEM),
        pl.BlockSpec(memory_space=pl.ANY),
    ]
    args = [n_valid, idx, idx, src]
    if weighted:
        in_specs += [pl.BlockSpec((rows, LANES), lambda i: (i, 0)), pl.BlockSpec((rows, d), lambda i: (i, 0))]
        args += [gates, base]
    return pl.pallas_call(
        functools.partial(_gather_kernel, rows=rows, groups=groups, weighted=weighted),
        grid=(steps,),
        in_specs=in_specs,
        out_specs=pl.BlockSpec((rows, d), lambda i: (i, 0)),
        out_shape=jax.ShapeDtypeStruct((steps * rows, d), out_dtype),
        scratch_shapes=[pltpu.VMEM((2, total, d), F32), pltpu.SemaphoreType.DMA((2,))],
        compiler_params=_params(("arbitrary",), 48),
        name="combine_rows" if weighted else "gather_rows",
    )(*args)


def _zero_unused_block(nu_ref, o_ref):
    @pl.when(pl.program_id(1) >= nu_ref[0])
    def _():
        o_ref[...] = jnp.zeros(o_ref.shape, o_ref.dtype)


def _moe1_kernel(be_ref, nu_ref, first_ref, x_ref, w_ref, bg_ref, bl_ref, o_ref, wp_ref, *, tn):
    i = pl.program_id(1)
    groups = tn // MXU_DIM
    half = MXU_DIM // 2
    _zero_unused_block(nu_ref, o_ref)

    @pl.when(first_ref[i] == 1)
    def _():
        rr = lax.broadcasted_iota(jnp.int32, (MXU_DIM, MXU_DIM), 0)
        cc = lax.broadcasted_iota(jnp.int32, (MXU_DIM, MXU_DIM), 1)
        src = jnp.where(cc < half, 2 * cc, 2 * (cc - half) + 1)
        perm = (rr == src).astype(BF16)
        for g in range(groups):
            sl = slice(g * MXU_DIM, (g + 1) * MXU_DIM)
            wt = w_ref[0, :, sl].astype(BF16)
            wp_ref[:, sl] = jnp.dot(wt, perm, preferred_element_type=F32).astype(BF16)

    @pl.when(i < nu_ref[0])
    def _():
        h = jnp.dot(x_ref[...], wp_ref[...], preferred_element_type=F32)
        for g in range(groups):
            fs = slice(g * half, (g + 1) * half)
            hg = h[:, g * MXU_DIM:g * MXU_DIM + half] + bg_ref[0, :, fs]
            hl = h[:, g * MXU_DIM + half:(g + 1) * MXU_DIM] + bl_ref[0, :, fs]
            glu = jnp.minimum(hg, SWIGLU_LIMIT)
            lin = jnp.clip(hl, -SWIGLU_LIMIT, SWIGLU_LIMIT)
            o_ref[:, fs] = (glu * jax.nn.sigmoid(SWIGLU_ALPHA * glu) * (lin + 1.0)).astype(o_ref.dtype)


def _moe2_kernel(be_ref, nu_ref, first_ref, a_ref, w_ref, b_ref, o_hbm, wb_ref, obuf, osem, *, bm):
    i = pl.program_id(1)
    slot = i % 2
    _rows_out_reclaim(i, slot, obuf, o_hbm, osem, bm)

    @pl.when(first_ref[i] == 1)
    def _():
        wb_ref[...] = w_ref[0].astype(BF16)

    @pl.when(i < nu_ref[0])
    def _():
        obuf[slot] = jnp.dot(a_ref[...], wb_ref[...], preferred_element_type=F32) + b_ref[0]

    @pl.when(i >= nu_ref[0])
    def _():
        obuf[slot] = jnp.zeros(obuf.shape[1:], obuf.dtype)

    _rows_out_send(i, pl.num_programs(1), slot, obuf, o_hbm, osem, bm)


def _grouped(kernel, name, rows_in, weight, biases, sched, bm, tn, out_cols_per_tile, out_dtype, token_rows_out=False):
    p, k = rows_in.shape
    n = weight.shape[2]
    nblk = p // bm
    tb = out_cols_per_tile
    blk = lambda i, nu: jnp.maximum(jnp.minimum(i, nu[0] - 1), 0)
    scratch = [pltpu.VMEM((k, tn), BF16)]
    if token_rows_out:
        assert n == tn == tb
        out_spec = pl.BlockSpec(memory_space=pl.ANY)
        out_shape = jax.ShapeDtypeStruct((p, 1, n), out_dtype)
        scratch += [pltpu.VMEM((2, bm, n), out_dtype), pltpu.SemaphoreType.DMA((2,))]
    else:
        out_spec = pl.BlockSpec((bm, tb), lambda j, i, be, nu, fi: (i, j))
        out_shape = jax.ShapeDtypeStruct((p, (n // tn) * tb), out_dtype)
    grid_spec = pltpu.PrefetchScalarGridSpec(
        num_scalar_prefetch=3,
        grid=(n // tn, nblk),
        in_specs=[pl.BlockSpec((bm, k), lambda j, i, be, nu, fi: (blk(i, nu), 0)),
                  pl.BlockSpec((1, k, tn), lambda j, i, be, nu, fi: (be[i], 0, j))]
        + [pl.BlockSpec((1, 1, tb), lambda j, i, be, nu, fi: (be[i], 0, j))] * len(biases),
        out_specs=out_spec,
        scratch_shapes=scratch,
    )
    return pl.pallas_call(
        kernel,
        grid_spec=grid_spec,
        out_shape=out_shape,
        compiler_params=_params(("arbitrary", "arbitrary"), 60),
        name=name,
    )(*sched, rows_in, weight, *biases)


def _moe(x1, xn, ti, tg, pos, cnt, w1, b1, w2, b2):
    t_tok, d = x1.shape
    n_exp, _, f2 = w1.shape
    f = f2 // 2
    bm = MOE_ROWS
    assert (t_tok * TOP_K) % bm == 0
    nblk = t_tok * TOP_K // bm + n_exp
    p = nblk * bm

    ti4 = ti[:, :TOP_K]
    counts = cnt[0, :n_exp]
    padded = (counts + bm - 1) // bm * bm
    pad_ends = jnp.cumsum(padded)
    pad_starts = pad_ends - padded
    onehot = ti4[:, :, None] == jnp.arange(n_exp, dtype=jnp.int32)[None, None, :]
    dest4 = jnp.sum(jnp.where(onehot, pad_starts[None, None, :], 0), axis=-1) + pos[:, :TOP_K]
    n_used = (pad_ends[-1] // bm).astype(jnp.int32).reshape(1)
    blk_start = jnp.arange(nblk, dtype=jnp.int32) * bm
    blk_start = jnp.minimum(blk_start, pad_ends[-1] - bm)
    blk_expert = jnp.sum(blk_start[:, None] >= pad_ends[None, :], axis=1).astype(jnp.int32)
    blk_expert = jnp.minimum(blk_expert, n_exp - 1)
    first = jnp.concatenate([jnp.ones((1,), jnp.int32), (blk_expert[1:] != blk_expert[:-1]).astype(jnp.int32)])
    sched = (blk_expert, n_used, first)
    tok = jnp.broadcast_to(jnp.arange(t_tok, dtype=jnp.int32)[:, None], (t_tok, TOP_K))
    row_tok = (jnp.arange(p, dtype=jnp.int32) % t_tok).at[dest4.reshape(-1)].set(
        tok.reshape(-1), unique_indices=True, mode="promise_in_bounds")

    gr = _tile(bm, GATHER_ROWS)
    assert bm % gr == 0
    xg = _gather_rows(xn, row_tok.reshape(p // gr, 1, gr), n_used * (bm // gr), gr, BF16)

    b1g = b1[:, 0::2].reshape(n_exp, 1, f).astype(F32)
    b1l = b1[:, 1::2].reshape(n_exp, 1, f).astype(F32)
    tn1 = _tile(f2, 2048)
    assert tn1 % MXU_DIM == 0
    act = _grouped(functools.partial(_moe1_kernel, tn=tn1), "expert_up", xg, w1, [b1g, b1l], sched,
                   bm, tn1, tn1 // 2, BF16)
    out = _grouped(functools.partial(_moe2_kernel, bm=bm), "expert_down", act, w2,
                   [b2.reshape(n_exp, 1, d).astype(F32)], sched, bm, d, d, F32, token_rows_out=True)

    tc = _tile(t_tok, COMBINE_TOKENS)
    steps = t_tok // tc
    idx = jnp.transpose(dest4.reshape(steps, tc, TOP_K), (0, 2, 1)).reshape(steps, 1, TOP_K * tc)
    return _gather_rows(out, idx, jnp.full((1,), steps, jnp.int32), tc, F32, gates=tg, base=x1)


def kernel(x, ln_mix, w_in, b_gate, q_gain, k_gain, lambda_q1, lambda_k1, lambda_q2, lambda_k2, sub_gain, rel_table, conv_w, conv_b, lru_wa, lru_ba, lru_wx, lru_bx, lru_lambda, w_branch_a, w_branch_b, w_out, ln_ffn, w_router, b_router, w1, b1, w2, b2):
    b, s_len, d = x.shape
    t_tok = b * s_len
    depth = ln_mix.shape[0]
    assert depth == 1
    q_cols = A_HEADS * 2 * A_QK_DIM
    qkv_cols = 2 * q_cols + A_HEADS * A_V_DIM
    lru_w = d // 2
    xt = x.reshape(t_tok, d)
    l = 0

    hn = _rmsnorm(xt, ln_mix[l], BF16)
    w_in_b = w_in[l].astype(BF16)
    qkv = _proj(hn, w_in_b, 0, qkv_cols, BF16)
    lru = _proj(hn, w_in_b, qkv_cols, 2 * lru_w, F32)
    gates = _proj(hn, w_in_b, qkv_cols + 2 * lru_w, 2 * d, BF16, bias=b_gate[l])

    lam = (jnp.exp(jnp.sum(lambda_q1[l].astype(F32) * lambda_k1[l].astype(F32)))
           - jnp.exp(jnp.sum(lambda_q2[l].astype(F32) * lambda_k2[l].astype(F32))) + LAM_INIT)
    o_a = _diff_attention(qkv.reshape(b, s_len, qkv_cols), rel_table, lam, q_gain[l], k_gain[l], sub_gain[l])
    o_b = _lru_branch(lru.reshape(b, s_len, 2 * lru_w), conv_w[l], conv_b[l], lru_wa[l], lru_ba[l],
                      lru_wx[l], lru_bx[l], lru_lambda[l])

    mixed = _merge(o_a.reshape(t_tok, -1), o_b.reshape(t_tok, lru_w),
                   w_branch_a[l].astype(BF16), w_branch_b[l].astype(BF16), gates)
    x1, xn, xn_rows = _outproj(mixed, w_out[l].astype(BF16), xt, ln_ffn[l])
    ti, tg, pos, cnt = _router(xn, w_router[l], b_router[l])
    y = _moe(x1, xn_rows, ti, tg, pos, cnt, w1[l], b1[l], w2[l], b2[l])
    return y.reshape(b, s_len, d)
```

```python
import functools
import math

import jax
import jax.numpy as jnp
from jax import lax
from jax.experimental import pallas as pl
from jax.experimental.pallas import tpu as pltpu

F32 = jnp.float32
BF16 = jnp.bfloat16

CHUNK = 64
RMS_EPS = 1e-6
A_HEADS = 8
A_QK_DIM = 64
A_V_DIM = 2 * A_QK_DIM
LRU_BLOCKS = 8
CONV_WIDTH = 4
LRU_C = 8.0
REL_BUCKETS = 32
REL_MAX_DIST = 128
TOP_K = 4
SWIGLU_LIMIT = 7.0
SWIGLU_ALPHA = 1.702
LAM_INIT = 0.8 - 0.6 * math.exp(-0.3 * 0)
LOG2E = 1.4426950408889634

LANES = 128
SUBLANES = 8
MXU_DIM = 256
NEG_BIG = -1e30
NEG_SEL = -3e38

ATT_Q_BLOCK = 512
ATT_ROW_GROUP = 128
MOE_ROWS = 512
EXPERT_UP_COLS = 2048
GATHER_ROWS = 512
COMBINE_TOKENS = 128


def _tile(n, pref):
    t = min(n, pref)
    assert n % t == 0, (n, pref)
    return t


def _params(semantics, vmem_mib):
    return pltpu.CompilerParams(dimension_semantics=semantics, vmem_limit_bytes=vmem_mib * 1024 * 1024)


def _rows_out_copy(step, slot, buf, o_hbm, sem, rows):
    return pltpu.make_async_copy(buf.at[slot], o_hbm.at[pl.ds(step * rows, rows), 0, :], sem.at[slot])


def _rows_out_reclaim(i, slot, buf, o_hbm, sem, rows):
    @pl.when(i >= 2)
    def _():
        _rows_out_copy(i - 2, slot, buf, o_hbm, sem, rows).wait()


def _rows_out_send(i, n, slot, buf, o_hbm, sem, rows):
    _rows_out_copy(i, slot, buf, o_hbm, sem, rows).start()

    @pl.when(i == n - 1)
    def _():
        @pl.when(n > 1)
        def _():
            _rows_out_copy(i - 1, 1 - slot, buf, o_hbm, sem, rows).wait()
        _rows_out_copy(i, slot, buf, o_hbm, sem, rows).wait()


def _rmsnorm_kernel(x_ref, g_ref, o_ref):
    x = x_ref[...]
    y = x * lax.rsqrt(jnp.mean(x * x, axis=-1, keepdims=True) + RMS_EPS)
    o_ref[...] = (y * g_ref[...]).astype(o_ref.dtype)


def _rmsnorm(x, g, out_dtype):
    m, d = x.shape
    tm = _tile(m, 512)
    return pl.pallas_call(
        _rmsnorm_kernel,
        grid=(m // tm,),
        in_specs=[pl.BlockSpec((tm, d), lambda i: (i, 0)), pl.BlockSpec((1, d), lambda i: (0, 0))],
        out_specs=pl.BlockSpec((tm, d), lambda i: (i, 0)),
        out_shape=jax.ShapeDtypeStruct((m, d), out_dtype),
        compiler_params=_params(("parallel",), 32),
        name="rmsnorm",
    )(x, g.reshape(1, d))


def _proj_kernel(a_ref, w_ref, *rest, sigmoid_bias):
    if sigmoid_bias:
        b_ref, o_ref = rest
    else:
        (o_ref,) = rest
    acc = jnp.dot(a_ref[...], w_ref[...], preferred_element_type=F32)
    if sigmoid_bias:
        acc = jax.nn.sigmoid(acc + b_ref[...])
    o_ref[...] = acc.astype(o_ref.dtype)


def _proj(a, w, col0, n, out_dtype, bias=None, tm=1024, tn=1024):
    m, k = a.shape
    tm, tn = _tile(m, tm), math.gcd(_tile(n, tn), col0)
    assert n % tn == 0 and tn % LANES == 0
    j0 = col0 // tn
    in_specs = [pl.BlockSpec((tm, k), lambda i, j: (i, 0)), pl.BlockSpec((k, tn), lambda i, j: (0, j + j0))]
    args = [a, w]
    if bias is not None:
        in_specs.append(pl.BlockSpec((1, tn), lambda i, j: (0, j)))
        args.append(bias.reshape(1, n))
    return pl.pallas_call(
        functools.partial(_proj_kernel, sigmoid_bias=bias is not None),
        grid=(m // tm, n // tn),
        in_specs=in_specs,
        out_specs=pl.BlockSpec((tm, tn), lambda i, j: (i, j)),
        out_shape=jax.ShapeDtypeStruct((m, n), out_dtype),
        compiler_params=_params(("parallel", "parallel"), 48),
        name="proj",
    )(*args)


def _t5_bucket(rel):
    nb = REL_BUCKETS // 2
    max_exact = nb // 2
    ret = jnp.where(rel > 0, nb, 0)
    n = jnp.abs(rel)
    nf = jnp.maximum(n, 1).astype(F32)
    large = max_exact + (jnp.log(nf / max_exact) / math.log(REL_MAX_DIST / max_exact)
                         * (nb - max_exact)).astype(jnp.int32)
    large = jnp.minimum(large, nb - 1)
    return ret + jnp.where(n < max_exact, n, large)


def _near_bias(rel_table, t):
    assert t + 1 >= REL_MAX_DIST
    table = rel_table.astype(F32)
    r = jnp.arange(t, dtype=jnp.int32)[:, None]
    c = jnp.arange(t, dtype=jnp.int32)[None, :]

    def lookup(bucket):
        out = jnp.zeros((A_HEADS,) + bucket.shape, F32)
        for bkt in range(REL_BUCKETS):
            out = jnp.where(bucket[None] == bkt, table[bkt][:, None, None], out)
        return out

    far = table[_t5_bucket(jnp.int32(-(t + 1)))][:, None, None]
    diag = lookup(_t5_bucket(c - r)) - far
    allowed = (c // CHUNK) <= (r // CHUNK)
    diag = jnp.where(allowed[None], diag * LOG2E, NEG_BIG)
    sub = (lookup(_t5_bucket(c - r - t)) - far) * LOG2E
    return jnp.stack([jnp.zeros_like(diag), sub, diag], axis=1)


def _attn_kernel(lam_ref, qa_ref, qb_ref, k_ref, v_ref, nb_ref, qg_ref, kg_ref, sg_ref, oa_ref, ob_ref,
                 kn_ref, vx_ref, m_ref, acc_ref, qs_ref, s_ref, *, t, nq, s_len, k_chunk, rg):
    g = pl.program_id(2)
    blk_a = g
    blk_b = nq - 1 - g
    hw = 2 * A_QK_DIM
    rows = 2 * t
    lo = lax.broadcasted_iota(jnp.int32, (1, hw), 1) < A_QK_DIM

    def qk_norm(x, gain):
        sq = x * x
        s_lo = jnp.sum(jnp.where(lo, sq, 0.0), axis=-1, keepdims=True)
        s_hi = jnp.sum(jnp.where(lo, 0.0, sq), axis=-1, keepdims=True)
        ms = jnp.where(lo, s_lo, s_hi) * (1.0 / A_QK_DIM)
        return x * lax.rsqrt(ms + RMS_EPS) * gain

    @pl.when(g == 0)
    def _():
        def body(c, carry):
            r0 = pl.multiple_of(c * k_chunk, k_chunk)
            kk = k_ref[0, pl.ds(r0, k_chunk), :].astype(F32)
            kn_ref[pl.ds(r0, k_chunk), :] = qk_norm(kk, kg_ref[...]).astype(BF16)
            vx_ref[pl.ds(r0, k_chunk), 0:hw] = v_ref[0, pl.ds(r0, k_chunk), :]
            vx_ref[pl.ds(r0, k_chunk), hw:2 * hw] = jnp.ones((k_chunk, hw), BF16)
            return carry
        lax.fori_loop(0, s_len // k_chunk, body, 0)

    for which, q_ref in ((0, qa_ref), (1, qb_ref)):
        q = qk_norm(q_ref[0].astype(F32), qg_ref[...]) * (A_QK_DIM ** -0.5 * LOG2E)
        qs_ref[which] = jnp.concatenate([jnp.where(lo, q, 0.0), jnp.where(lo, 0.0, q)], axis=0).astype(BF16)

    m_ref[...] = jnp.full(m_ref.shape, NEG_BIG, F32)
    acc_ref[...] = jnp.zeros(acc_ref.shape, F32)

    groups = [slice(c * rg, (c + 1) * rg) for c in range(rows // rg)]

    def item(i):
        which = (i > blk_a).astype(jnp.int32)
        qblk = jnp.where(i > blk_a, blk_b, blk_a)
        return which, qblk, i - which * (blk_a + 1)

    def logits(i, slot):
        which, qblk, kb = item(i)
        kj = kn_ref[pl.ds(pl.multiple_of(kb * t, t), t), :]
        tile = jnp.clip(kb - (qblk - 2), 0, 2)
        for rs in groups:
            s = lax.dot_general(qs_ref[which, rs, :], kj, (((1,), (1,)), ((), ())), preferred_element_type=F32)
            b0 = rs.start % t
            s_ref[slot, rs, :] = s + nb_ref[0, tile, b0:b0 + rg, :]

    def accumulate(i, slot):
        which, _, kb = item(i)
        vj = vx_ref[pl.ds(pl.multiple_of(kb * t, t), t), :]
        for rs in groups:
            s = s_ref[slot, rs, :]
            chunks = [s[:, c * LANES:(c + 1) * LANES] for c in range(t // LANES)]
            mc = chunks[0]
            for ch in chunks[1:]:
                mc = jnp.maximum(mc, ch)
            m_prev = m_ref[which, rs, :]
            m_new = jnp.maximum(m_prev, jnp.max(mc, axis=-1, keepdims=True))
            alpha = jnp.exp2(m_prev - m_new)
            p = jnp.concatenate([jnp.exp2(ch - m_new) for ch in chunks], axis=1).astype(BF16)
            pv = jnp.dot(p, vj, preferred_element_type=F32)
            acc_ref[which, rs, :] = jnp.concatenate([alpha, alpha], axis=1) * acc_ref[which, rs, :] + pv
            m_ref[which, rs, :] = m_new

    logits(0, 0)
    for i in range(nq + 1):
        accumulate(i, i % 2)
        if i < nq:
            logits(i + 1, (i + 1) % 2)

    lam = lam_ref[0]
    for which, o_ref in ((0, oa_ref), (1, ob_ref)):
        acc = acc_ref[which]
        o = acc[:t, :hw] / acc[:t, hw:] - lam * (acc[t:, :hw] / acc[t:, hw:])
        o = o * lax.rsqrt(jnp.mean(o * o, axis=-1, keepdims=True) + RMS_EPS) * sg_ref[...]
        o_ref[0] = (o * (1.0 - LAM_INIT)).astype(o_ref.dtype)


def _diff_attention(qkv, rel_table, lam, q_gain, k_gain, sub_gain):
    b, s_len, _ = qkv.shape
    t = _tile(s_len, ATT_Q_BLOCK)
    nq = s_len // t
    assert nq % 2 == 0
    half = nq // 2
    hw = 2 * A_QK_DIM
    nb = _near_bias(rel_table, t)
    tile2 = lambda gain: jnp.concatenate([gain, gain]).reshape(1, hw).astype(F32)
    kern = functools.partial(_attn_kernel, t=t, nq=nq, s_len=s_len, k_chunk=_tile(s_len, 512),
                             rg=_tile(t, ATT_ROW_GROUP))
    const = pl.BlockSpec((1, hw), lambda bi, h, g: (0, 0))
    o_lo, o_hi = pl.pallas_call(
        kern,
        grid=(b, A_HEADS, half),
        in_specs=[
            pl.BlockSpec(memory_space=pltpu.SMEM),
            pl.BlockSpec((1, t, hw), lambda bi, h, g: (bi, g, h)),
            pl.BlockSpec((1, t, hw), lambda bi, h, g: (bi, nq - 1 - g, h)),
            pl.BlockSpec((1, s_len, hw), lambda bi, h, g: (bi, 0, A_HEADS + h)),
            pl.BlockSpec((1, s_len, hw), lambda bi, h, g: (bi, 0, 2 * A_HEADS + h)),
            pl.BlockSpec((1, 3, t, t), lambda bi, h, g: (h, 0, 0, 0)),
            const, const, const,
        ],
        out_specs=[pl.BlockSpec((1, t, hw), lambda bi, h, g: (bi, g, h)),
                   pl.BlockSpec((1, t, hw), lambda bi, h, g: (bi, half - 1 - g, h))],
        out_shape=[jax.ShapeDtypeStruct((b, s_len // 2, A_HEADS * hw), BF16)] * 2,
        scratch_shapes=[
            pltpu.VMEM((s_len, hw), BF16),
            pltpu.VMEM((s_len, 2 * hw), BF16),
            pltpu.VMEM((2, 2 * t, hw), F32),
            pltpu.VMEM((2, 2 * t, 2 * hw), F32),
            pltpu.VMEM((2, 2 * t, hw), BF16),
            pltpu.VMEM((2, 2 * t, t), F32),
        ],
        compiler_params=_params(("arbitrary", "arbitrary", "arbitrary"), 48),
        name="diff_attention",
    )(lam.reshape(1).astype(F32), qkv, qkv, qkv, qkv, nb, tile2(q_gain), tile2(k_gain),
      sub_gain.reshape(1, hw).astype(F32))
    return jnp.concatenate([o_lo, o_hi], axis=1)


def _lru_kernel(xl_ref, gl_ref, cw_ref, cb_ref, wa_ref, ba_ref, wx_ref, bx_ref, c_ref, o_ref,
                xbuf, a_s, u_s, h_s, hc, *, ts, width):
    i = pl.program_id(1)
    halo = SUBLANES
    bd = width // LRU_BLOCKS

    @pl.when(i == 0)
    def _():
        xbuf[0:halo, :] = jnp.zeros((halo, width), F32)
        hc[...] = jnp.zeros(hc.shape, F32)

    x = xl_ref[0]
    xbuf[halo:halo + ts, :] = x
    xr = cb_ref[...] + cw_ref[0:1, :] * xbuf[pl.ds(halo - 3, ts), :]
    for j in range(1, CONV_WIDTH):
        xr = xr + cw_ref[j:j + 1, :] * xbuf[pl.ds(halo - 3 + j, ts), :]
    xbuf[0:halo, :] = x[ts - halo:, :]

    for g in range(LRU_BLOCKS):
        sl = slice(g * bd, (g + 1) * bd)
        xg = xr[:, sl]
        xb = xg.astype(BF16)
        r = jax.nn.sigmoid(jnp.dot(xb, wa_ref[g], preferred_element_type=F32) + ba_ref[:, sl])
        gi = jax.nn.sigmoid(jnp.dot(xb, wx_ref[g], preferred_element_type=F32) + bx_ref[:, sl])
        log_a = r * c_ref[:, sl]
        a = jnp.exp(log_a)
        a_s[:, sl] = a
        u_s[:, sl] = jnp.sqrt(-jnp.tanh(log_a) * (1.0 + a * a)) * (gi * xg)

    row = lax.broadcasted_iota(jnp.int32, (SUBLANES, width), 0)

    def body(gidx, h):
        r0 = pl.multiple_of(gidx * SUBLANES, SUBLANES)
        a = a_s[pl.ds(r0, SUBLANES), :]
        u = u_s[pl.ds(r0, SUBLANES), :]
        for d in (1, 2, 4):
            keep = row >= d
            u = jnp.where(keep, a * pltpu.roll(u, d, 0) + u, u)
            a = jnp.where(keep, a * pltpu.roll(a, d, 0), a)
        hr = a * h + u
        h_s[pl.ds(r0, SUBLANES), :] = hr
        return hr[SUBLANES - 1:SUBLANES, :]

    hc[...] = lax.fori_loop(0, ts // SUBLANES, body, hc[...], unroll=2)
    o_ref[0] = (h_s[...] * jax.nn.gelu(gl_ref[0])).astype(o_ref.dtype)


def _lru_branch(lru, conv_w, conv_b, wa, ba, wx, bx, lam_param):
    b, s_len, c2 = lru.shape
    width = c2 // 2
    ts = _tile(s_len, 256)
    bd = width // LRU_BLOCKS
    c_vec = (-LRU_C * jax.nn.softplus(-lam_param.astype(F32))).reshape(1, width)
    row = lambda v: v.reshape(1, width).astype(F32)
    const2 = lambda shape: pl.BlockSpec(shape, lambda bi, i: (0,) * len(shape))
    kern = functools.partial(_lru_kernel, ts=ts, width=width)
    return pl.pallas_call(
        kern,
        grid=(b, s_len // ts),
        in_specs=[
            pl.BlockSpec((1, ts, width), lambda bi, i: (bi, i, 0)),
            pl.BlockSpec((1, ts, width), lambda bi, i: (bi, i, 1)),
            const2((CONV_WIDTH, width)), const2((1, width)),
            const2((LRU_BLOCKS, bd, bd)), const2((1, width)),
            const2((LRU_BLOCKS, bd, bd)), const2((1, width)),
            const2((1, width)),
        ],
        out_specs=pl.BlockSpec((1, ts, width), lambda bi, i: (bi, i, 0)),
        out_shape=jax.ShapeDtypeStruct((b, s_len, width), BF16),
        scratch_shapes=[
            pltpu.VMEM((ts + SUBLANES, width), F32),
            pltpu.VMEM((ts, width), F32),
            pltpu.VMEM((ts, width), F32),
            pltpu.VMEM((ts, width), F32),
            pltpu.VMEM((1, width), F32),
        ],
        compiler_params=_params(("arbitrary", "arbitrary"), 40),
        name="rg_lru",
    )(lru, lru, conv_w.astype(F32), row(conv_b), wa.astype(BF16), row(ba), wx.astype(BF16), row(bx), c_vec)


def _merge_kernel(oa_ref, ob_ref, wa_ref, wb_ref, g0_ref, g1_ref, o_ref):
    ya = jnp.dot(oa_ref[...], wa_ref[...], preferred_element_type=F32)
    yb = jnp.dot(ob_ref[...], wb_ref[...], preferred_element_type=F32)
    o_ref[...] = (g0_ref[...].astype(F32) * ya + g1_ref[...].astype(F32) * yb).astype(o_ref.dtype)


def _merge(o_a, o_b, w_a, w_b, gates):
    m, ka = o_a.shape
    kb = o_b.shape[1]
    d = w_a.shape[1]
    tm, tn = _tile(m, 512), _tile(d, 1024)
    nj = d // tn
    return pl.pallas_call(
        _merge_kernel,
        grid=(m // tm, nj),
        in_specs=[
            pl.BlockSpec((tm, ka), lambda i, j: (i, 0)),
            pl.BlockSpec((tm, kb), lambda i, j: (i, 0)),
            pl.BlockSpec((ka, tn), lambda i, j: (0, j)),
            pl.BlockSpec((kb, tn), lambda i, j: (0, j)),
            pl.BlockSpec((tm, tn), lambda i, j: (i, j)),
            pl.BlockSpec((tm, tn), lambda i, j: (i, j + nj)),
        ],
        out_specs=pl.BlockSpec((tm, tn), lambda i, j: (i, j)),
        out_shape=jax.ShapeDtypeStruct((m, d), BF16),
        compiler_params=_params(("parallel", "parallel"), 40),
        name="branch_merge",
    )(o_a, o_b, w_a, w_b, gates, gates)


def _outproj_kernel(mx_ref, wo_ref, x_ref, g_ref, x1_ref, xn_ref, xn_hbm, nbuf, nsem, *, tm):
    i = pl.program_id(0)
    slot = i % 2
    _rows_out_reclaim(i, slot, nbuf, xn_hbm, nsem, tm)
    x1 = x_ref[...] + jnp.dot(mx_ref[...], wo_ref[...], preferred_element_type=F32)
    x1_ref[...] = x1
    xn = x1 * lax.rsqrt(jnp.mean(x1 * x1, axis=-1, keepdims=True) + RMS_EPS) * g_ref[...]
    xn_ref[...] = xn
    nbuf[slot] = xn
    _rows_out_send(i, pl.num_programs(0), slot, nbuf, xn_hbm, nsem, tm)


def _outproj(mixed, w_out, x, ln_ffn):
    m, d = x.shape
    tm = _tile(m, 512)
    row_blk = pl.BlockSpec((tm, d), lambda i: (i, 0))
    return pl.pallas_call(
        functools.partial(_outproj_kernel, tm=tm),
        grid=(m // tm,),
        in_specs=[row_blk, pl.BlockSpec((d, d), lambda i: (0, 0)), row_blk, pl.BlockSpec((1, d), lambda i: (0, 0))],
        out_specs=[row_blk, row_blk, pl.BlockSpec(memory_space=pl.ANY)],
        out_shape=[jax.ShapeDtypeStruct((m, d), F32), jax.ShapeDtypeStruct((m, d), F32),
                   jax.ShapeDtypeStruct((m, 1, d), F32)],
        scratch_shapes=[pltpu.VMEM((2, tm, d), F32), pltpu.SemaphoreType.DMA((2,))],
        compiler_params=_params(("arbitrary",), 60),
        name="outproj",
    )(mixed, w_out, x, ln_ffn.reshape(1, d).astype(F32))


def _router_kernel(xn_ref, w2_ref, wh_ref, br_ref, ti_ref, tg_ref, pos_ref, cnt_ref, carry, *, tm, sub):
    i = pl.program_id(0)

    @pl.when(i == 0)
    def _():
        carry[...] = jnp.zeros(carry.shape, F32)

    lane = lax.broadcasted_iota(jnp.int32, (sub, LANES), 1)
    rr = lax.broadcasted_iota(jnp.int32, (sub, sub), 0)
    cc = lax.broadcasted_iota(jnp.int32, (sub, sub), 1)
    below = (cc < rr).astype(BF16)

    for c in range(tm // sub):
        rs = slice(c * sub, (c + 1) * sub)
        xn = xn_ref[rs, :]
        xh = xn.astype(BF16)
        xl = (xn - xh.astype(F32)).astype(BF16)
        hi = jnp.dot(xh, w2_ref[...], preferred_element_type=F32)
        lo = jnp.dot(xl, wh_ref[...], preferred_element_type=F32)
        rest = hi[:, :LANES] + hi[:, LANES:] + lo + br_ref[...]

        vals, idxs = [], []
        for _ in range(TOP_K):
            mx = jnp.max(rest, axis=-1, keepdims=True)
            ix = jnp.min(jnp.where(rest == mx, lane, LANES), axis=-1, keepdims=True)
            vals.append(mx)
            idxs.append(ix)
            rest = jnp.where(lane == ix, NEG_SEL, rest)
        exps = [jnp.exp(v - vals[0]) for v in vals]
        den = exps[0]
        for e in exps[1:]:
            den = den + e
        ti = jnp.zeros((sub, LANES), jnp.int32)
        tg = jnp.zeros((sub, LANES), F32)
        sel = jnp.zeros((sub, LANES), F32)
        for k in range(TOP_K):
            ti = jnp.where(lane == k, idxs[k], ti)
            tg = jnp.where(lane == k, exps[k] / den, tg)
            sel = jnp.where(lane == idxs[k], 1.0, sel)
        ti_ref[rs, :] = ti
        tg_ref[rs, :] = tg

        pos = jnp.dot(below, sel.astype(BF16), preferred_element_type=F32) + carry[...]
        pos4 = jnp.zeros((sub, LANES), F32)
        for k in range(TOP_K):
            pk = jnp.sum(jnp.where(lane == idxs[k], pos, 0.0), axis=-1, keepdims=True)
            pos4 = jnp.where(lane == k, pk, pos4)
        pos_ref[rs, :] = pos4.astype(jnp.int32)
        carry[...] = carry[...] + jnp.sum(sel, axis=0, keepdims=True)
    cnt_ref[...] = carry[...].astype(jnp.int32)


def _router(xn, w_router, b_router):
    m, d = xn.shape
    e = w_router.shape[1]
    assert e <= LANES
    tm = _tile(m, 1024)
    wr = jnp.zeros((d, LANES), F32).at[:, :e].set(w_router.astype(F32))
    wh = wr.astype(BF16)
    wl = (wr - wh.astype(F32)).astype(BF16)
    br = jnp.full((1, LANES), NEG_BIG, F32).at[0, :e].set(b_router.astype(F32))
    row_blk = lambda w: pl.BlockSpec((tm, w), lambda i: (i, 0))
    const = lambda shape: pl.BlockSpec(shape, lambda i: (0, 0))
    return pl.pallas_call(
        functools.partial(_router_kernel, tm=tm, sub=_tile(tm, 256)),
        grid=(m // tm,),
        in_specs=[row_blk(d), const((d, 2 * LANES)), const((d, LANES)), const((1, LANES))],
        out_specs=[row_blk(LANES), row_blk(LANES), row_blk(LANES), const((1, LANES))],
        out_shape=[
            jax.ShapeDtypeStruct((m, LANES), jnp.int32),
            jax.ShapeDtypeStruct((m, LANES), F32),
            jax.ShapeDtypeStruct((m, LANES), jnp.int32),
            jax.ShapeDtypeStruct((1, LANES), jnp.int32),
        ],
        scratch_shapes=[pltpu.VMEM((1, LANES), F32)],
        compiler_params=_params(("arbitrary",), 40),
        name="router",
    )(xn, jnp.concatenate([wh, wl], axis=1), wh, br)


def _gather_kernel(nv_ref, idx_ref, nxt_ref, src_ref, *rest, rows, groups, weighted):
    if weighted:
        g_ref, base_ref, o_ref, buf, sem = rest
    else:
        o_ref, buf, sem = rest
    i = pl.program_id(0)
    n = nv_ref[0]
    total = rows * groups

    def row_copy(ref, r, slot):
        return pltpu.make_async_copy(src_ref.at[ref[0, 0, r]], buf.at[slot, pl.ds(r, 1), :], sem.at[slot])

    def issue(ref, slot):
        def body(r, carry):
            row_copy(ref, r, slot).start()
            return carry
        lax.fori_loop(0, total, body, 0, unroll=32)

    @pl.when(jnp.logical_and(i == 0, n > 0))
    def _():
        issue(idx_ref, 0)

    @pl.when(i + 1 < n)
    def _():
        issue(nxt_ref, (i + 1) % 2)

    slot = i % 2

    @pl.when(i < n)
    def _():
        def wait_body(r, carry):
            row_copy(idx_ref, r, slot).wait()
            return carry
        lax.fori_loop(0, total, wait_body, 0, unroll=8)

        if weighted:
            lane = lax.broadcasted_iota(jnp.int32, (rows, LANES), 1)
            g = g_ref[...]
            acc = base_ref[...]
            for k in range(groups):
                gk = jnp.sum(jnp.where(lane == k, g, 0.0), axis=-1, keepdims=True)
                acc = acc + gk * buf[slot, pl.ds(k * rows, rows), :]
            o_ref[...] = acc.astype(o_ref.dtype)
        else:
            o_ref[...] = buf[slot].astype(o_ref.dtype)

    @pl.when(i >= n)
    def _():
        o_ref[...] = jnp.zeros(o_ref.shape, o_ref.dtype)


def _gather_rows(src, idx, n_valid, rows, out_dtype, gates=None, base=None):
    steps, _, total = idx.shape
    groups = total // rows
    d = src.shape[2]
    weighted = gates is not None
    in_specs = [
        pl.BlockSpec(memory_space=pltpu.SMEM),
        pl.BlockSpec((1, 1, total), lambda i: (i, 0, 0), memory_space=pltpu.SMEM),
        pl.BlockSpec((1, 1, total), lambda i: (jnp.minimum(i + 1, steps - 1), 0, 0), memory_space=pltpu.SMEM),
        pl.BlockSpec(memory_space=pl.ANY),
    ]
    args = [n_valid, idx, idx, src]
    if weighted:
        in_specs += [pl.BlockSpec((rows, LANES), lambda i: (i, 0)), pl.BlockSpec((rows, d), lambda i: (i, 0))]
        args += [gates, base]
    return pl.pallas_call(
        functools.partial(_gather_kernel, rows=rows, groups=groups, weighted=weighted),
        grid=(steps,),
        in_specs=in_specs,
        out_specs=pl.BlockSpec((rows, d), lambda i: (i, 0)),
        out_shape=jax.ShapeDtypeStruct((steps * rows, d), out_dtype),
        scratch_shapes=[pltpu.VMEM((2, total, d), F32), pltpu.SemaphoreType.DMA((2,))],
        compiler_params=_params(("arbitrary",), 48),
        name="combine_rows" if weighted else "gather_rows",
    )(*args)


def _zero_unused_block(nu_ref, o_ref):
    @pl.when(pl.program_id(1) >= nu_ref[0])
    def _():
        o_ref[...] = jnp.zeros(o_ref.shape, o_ref.dtype)


def _weight_tiles(sched_refs, w_hbm, wbuf, wsem, tn, convert):
    be_ref, _, first_ref, run_ref, nruns_ref, nexte_ref, nic_ref = sched_refs
    j = pl.program_id(0)
    i = pl.program_id(1)
    nj = pl.num_programs(0)

    def tile_copy(e, jj, slot):
        return pltpu.make_async_copy(w_hbm.at[e, :, pl.ds(pl.multiple_of(jj * tn, tn), tn)],
                                     wbuf.at[slot], wsem.at[slot])

    slot = (j * nruns_ref[0] + run_ref[i]) % 2

    @pl.when(jnp.logical_and(i == 0, j == 0))
    def _():
        tile_copy(be_ref[0], 0, 0).start()

    @pl.when(first_ref[i] == 1)
    def _():
        tile_copy(be_ref[i], j, slot).wait()
        convert(wbuf.at[slot])
        in_col = nic_ref[i] == 1

        @pl.when(jnp.logical_or(in_col, j + 1 < nj))
        def _():
            tile_copy(nexte_ref[i], jnp.where(in_col, j, j + 1), 1 - slot).start()


def _moe1_kernel(*refs, tn):
    sched_refs, (x_ref, w_hbm, bg_ref, bl_ref, o_ref, wp_ref, wbuf, wsem) = refs[:7], refs[7:]
    nu_ref = sched_refs[1]
    i = pl.program_id(1)
    groups = tn // MXU_DIM
    half = MXU_DIM // 2
    _zero_unused_block(nu_ref, o_ref)

    def convert(w_ref):
        rr = lax.broadcasted_iota(jnp.int32, (MXU_DIM, MXU_DIM), 0)
        cc = lax.broadcasted_iota(jnp.int32, (MXU_DIM, MXU_DIM), 1)
        src = jnp.where(cc < half, 2 * cc, 2 * (cc - half) + 1)
        perm = (rr == src).astype(BF16)
        for g in range(groups):
            sl = slice(g * MXU_DIM, (g + 1) * MXU_DIM)
            wt = w_ref[:, sl].astype(BF16)
            wp_ref[:, sl] = jnp.dot(wt, perm, preferred_element_type=F32).astype(BF16)

    _weight_tiles(sched_refs, w_hbm, wbuf, wsem, tn, convert)

    @pl.when(i < nu_ref[0])
    def _():
        h = jnp.dot(x_ref[...], wp_ref[...], preferred_element_type=F32)
        for g in range(groups):
            fs = slice(g * half, (g + 1) * half)
            hg = h[:, g * MXU_DIM:g * MXU_DIM + half] + bg_ref[0, :, fs]
            hl = h[:, g * MXU_DIM + half:(g + 1) * MXU_DIM] + bl_ref[0, :, fs]
            glu = jnp.minimum(hg, SWIGLU_LIMIT)
            lin = jnp.clip(hl, -SWIGLU_LIMIT, SWIGLU_LIMIT)
            o_ref[:, fs] = (glu * jax.nn.sigmoid(SWIGLU_ALPHA * glu) * (lin + 1.0)).astype(o_ref.dtype)


def _moe2_kernel(*refs, bm, tn):
    sched_refs, (a_ref, w_hbm, b_ref, o_hbm, wb_ref, wbuf, wsem, obuf, osem) = refs[:7], refs[7:]
    nu_ref = sched_refs[1]
    i = pl.program_id(1)
    slot = i % 2
    _rows_out_reclaim(i, slot, obuf, o_hbm, osem, bm)

    def convert(w_ref):
        wb_ref[...] = w_ref[...].astype(BF16)

    _weight_tiles(sched_refs, w_hbm, wbuf, wsem, tn, convert)

    @pl.when(i < nu_ref[0])
    def _():
        obuf[slot] = jnp.dot(a_ref[...], wb_ref[...], preferred_element_type=F32) + b_ref[0]

    @pl.when(i >= nu_ref[0])
    def _():
        obuf[slot] = jnp.zeros(obuf.shape[1:], obuf.dtype)

    _rows_out_send(i, pl.num_programs(1), slot, obuf, o_hbm, osem, bm)


def _grouped(kernel, name, rows_in, weight, biases, sched, bm, tn, out_cols_per_tile, out_dtype, token_rows_out=False):
    p, k = rows_in.shape
    n = weight.shape[2]
    nblk = p // bm
    tb = out_cols_per_tile
    blk = lambda i, nu: jnp.maximum(jnp.minimum(i, nu[0] - 1), 0)
    scratch = [pltpu.VMEM((k, tn), BF16), pltpu.VMEM((2, k, tn), F32), pltpu.SemaphoreType.DMA((2,))]
    if token_rows_out:
        assert n == tn == tb
        out_spec = pl.BlockSpec(memory_space=pl.ANY)
        out_shape = jax.ShapeDtypeStruct((p, 1, n), out_dtype)
        scratch += [pltpu.VMEM((2, bm, n), out_dtype), pltpu.SemaphoreType.DMA((2,))]
    else:
        out_spec = pl.BlockSpec((bm, tb), lambda j, i, *_: (i, j))
        out_shape = jax.ShapeDtypeStruct((p, (n // tn) * tb), out_dtype)
    grid_spec = pltpu.PrefetchScalarGridSpec(
        num_scalar_prefetch=len(sched),
        grid=(n // tn, nblk),
        in_specs=[pl.BlockSpec((bm, k), lambda j, i, be, nu, *_: (blk(i, nu), 0)),
                  pl.BlockSpec(memory_space=pl.ANY)]
        + [pl.BlockSpec((1, 1, tb), lambda j, i, be, *_: (be[i], 0, j))] * len(biases),
        out_specs=out_spec,
        scratch_shapes=scratch,
    )
    return pl.pallas_call(
        kernel,
        grid_spec=grid_spec,
        out_shape=out_shape,
        compiler_params=_params(("arbitrary", "arbitrary"), 60),
        name=name,
    )(*sched, rows_in, weight, *biases)


def _moe(x1, xn, ti, tg, pos, cnt, w1, b1, w2, b2):
    t_tok, d = x1.shape
    n_exp, _, f2 = w1.shape
    f = f2 // 2
    bm = MOE_ROWS
    assert (t_tok * TOP_K) % bm == 0
    nblk = t_tok * TOP_K // bm + n_exp
    p = nblk * bm

    ti4 = ti[:, :TOP_K]
    counts = cnt[0, :n_exp]
    padded = (counts + bm - 1) // bm * bm
    pad_ends = jnp.cumsum(padded)
    pad_starts = pad_ends - padded
    onehot = ti4[:, :, None] == jnp.arange(n_exp, dtype=jnp.int32)[None, None, :]
    dest4 = jnp.sum(jnp.where(onehot, pad_starts[None, None, :], 0), axis=-1) + pos[:, :TOP_K]
    n_used = (pad_ends[-1] // bm).astype(jnp.int32).reshape(1)
    blk_start = jnp.arange(nblk, dtype=jnp.int32) * bm
    blk_start = jnp.minimum(blk_start, pad_ends[-1] - bm)
    blk_expert = jnp.sum(blk_start[:, None] >= pad_ends[None, :], axis=1).astype(jnp.int32)
    blk_expert = jnp.minimum(blk_expert, n_exp - 1)
    first = jnp.concatenate([jnp.ones((1,), jnp.int32), (blk_expert[1:] != blk_expert[:-1]).astype(jnp.int32)])
    blk_ids = jnp.arange(nblk, dtype=jnp.int32)
    run_idx = (jnp.cumsum(first) - 1).astype(jnp.int32)
    n_runs = (run_idx[jnp.maximum(n_used[0] - 1, 0)] + 1).reshape(1)
    first_pos = jnp.where((first == 1) & (blk_ids < n_used[0]), blk_ids, nblk)
    later_first = jnp.concatenate([lax.cummin(first_pos[::-1])[::-1][1:], jnp.full((1,), nblk, jnp.int32)])
    next_in_col = (later_first < nblk).astype(jnp.int32)
    next_expert = jnp.where(later_first < nblk, blk_expert[jnp.minimum(later_first, nblk - 1)], blk_expert[0])
    sched = (blk_expert, n_used, first, run_idx, n_runs, next_expert.astype(jnp.int32), next_in_col)
    tok = jnp.broadcast_to(jnp.arange(t_tok, dtype=jnp.int32)[:, None], (t_tok, TOP_K))
    row_tok = (jnp.arange(p, dtype=jnp.int32) % t_tok).at[dest4.reshape(-1)].set(
        tok.reshape(-1), unique_indices=True, mode="promise_in_bounds")

    gr = _tile(bm, GATHER_ROWS)
    assert bm % gr == 0
    xg = _gather_rows(xn, row_tok.reshape(p // gr, 1, gr), n_used * (bm // gr), gr, BF16)

    b1g = b1[:, 0::2].reshape(n_exp, 1, f).astype(F32)
    b1l = b1[:, 1::2].reshape(n_exp, 1, f).astype(F32)
    tn1 = _tile(f2, EXPERT_UP_COLS)
    assert tn1 % MXU_DIM == 0
    act = _grouped(functools.partial(_moe1_kernel, tn=tn1), "expert_up", xg, w1, [b1g, b1l], sched,
                   bm, tn1, tn1 // 2, BF16)
    out = _grouped(functools.partial(_moe2_kernel, bm=bm, tn=d), "expert_down", act, w2,
                   [b2.reshape(n_exp, 1, d).astype(F32)], sched, bm, d, d, F32, token_rows_out=True)

    tc = _tile(t_tok, COMBINE_TOKENS)
    steps = t_tok // tc
    idx = jnp.transpose(dest4.reshape(steps, tc, TOP_K), (0, 2, 1)).reshape(steps, 1, TOP_K * tc)
    return _gather_rows(out, idx, jnp.full((1,), steps, jnp.int32), tc, F32, gates=tg, base=x1)


def kernel(x, ln_mix, w_in, b_gate, q_gain, k_gain, lambda_q1, lambda_k1, lambda_q2, lambda_k2, sub_gain, rel_table, conv_w, conv_b, lru_wa, lru_ba, lru_wx, lru_bx, lru_lambda, w_branch_a, w_branch_b, w_out, ln_ffn, w_router, b_router, w1, b1, w2, b2):
    b, s_len, d = x.shape
    t_tok = b * s_len
    depth = ln_mix.shape[0]
    assert depth == 1
    q_cols = A_HEADS * 2 * A_QK_DIM
    qkv_cols = 2 * q_cols + A_HEADS * A_V_DIM
    lru_w = d // 2
    xt = x.reshape(t_tok, d)
    l = 0

    hn = _rmsnorm(xt, ln_mix[l], BF16)
    w_in_b = w_in[l].astype(BF16)
    qkv = _proj(hn, w_in_b, 0, qkv_cols, BF16)
    lru = _proj(hn, w_in_b, qkv_cols, 2 * lru_w, F32)
    gates = _proj(hn, w_in_b, qkv_cols + 2 * lru_w, 2 * d, BF16, bias=b_gate[l])

    lam = (jnp.exp(jnp.sum(lambda_q1[l].astype(F32) * lambda_k1[l].astype(F32)))
           - jnp.exp(jnp.sum(lambda_q2[l].astype(F32) * lambda_k2[l].astype(F32))) + LAM_INIT)
    o_a = _diff_attention(qkv.reshape(b, s_len, qkv_cols), rel_table, lam, q_gain[l], k_gain[l], sub_gain[l])
    o_b = _lru_branch(lru.reshape(b, s_len, 2 * lru_w), conv_w[l], conv_b[l], lru_wa[l], lru_ba[l],
                      lru_wx[l], lru_bx[l], lru_lambda[l])

    mixed = _merge(o_a.reshape(t_tok, -1), o_b.reshape(t_tok, lru_w),
                   w_branch_a[l].astype(BF16), w_branch_b[l].astype(BF16), gates)
    x1, xn, xn_rows = _outproj(mixed, w_out[l].astype(BF16), xt, ln_ffn[l])
    ti, tg, pos, cnt = _router(xn, w_router[l], b_router[l])
    y = _moe(x1, xn_rows, ti, tg, pos, cnt, w1[l], b1[l], w2[l], b2[l])
    return y.reshape(b, s_len, d)
```

```python
import functools
import math

import jax
import jax.numpy as jnp
from jax import lax
from jax.experimental import pallas as pl
from jax.experimental.pallas import tpu as pltpu

F32 = jnp.float32
BF16 = jnp.bfloat16

CHUNK = 64
RMS_EPS = 1e-6
A_HEADS = 8
A_QK_DIM = 64
A_V_DIM = 2 * A_QK_DIM
LRU_BLOCKS = 8
CONV_WIDTH = 4
LRU_C = 8.0
REL_BUCKETS = 32
REL_MAX_DIST = 128
TOP_K = 4
SWIGLU_LIMIT = 7.0
SWIGLU_ALPHA = 1.702
LAM_INIT = 0.8 - 0.6 * math.exp(-0.3 * 0)
LOG2E = 1.4426950408889634

LANES = 128
SUBLANES = 8
MXU_DIM = 256
NEG_BIG = -1e30
NEG_SEL = -3e38

ATT_Q_BLOCK = 512
ATT_ROW_GROUP = 128
MOE_ROWS = 512
EXPERT_UP_COLS = 2048
GATHER_ROWS = 512
COMBINE_TOKENS = 128


def _tile(n, pref):
    t = min(n, pref)
    assert n % t == 0, (n, pref)
    return t


def _params(semantics, vmem_mib):
    return pltpu.CompilerParams(dimension_semantics=semantics, vmem_limit_bytes=vmem_mib * 1024 * 1024)


def _rows_out_copy(step, slot, buf, o_hbm, sem, rows):
    return pltpu.make_async_copy(buf.at[slot], o_hbm.at[pl.ds(step * rows, rows), 0, :], sem.at[slot])


def _rows_out_reclaim(i, slot, buf, o_hbm, sem, rows):
    @pl.when(i >= 2)
    def _():
        _rows_out_copy(i - 2, slot, buf, o_hbm, sem, rows).wait()


def _rows_out_send(i, n, slot, buf, o_hbm, sem, rows):
    _rows_out_copy(i, slot, buf, o_hbm, sem, rows).start()

    @pl.when(i == n - 1)
    def _():
        @pl.when(n > 1)
        def _():
            _rows_out_copy(i - 1, 1 - slot, buf, o_hbm, sem, rows).wait()
        _rows_out_copy(i, slot, buf, o_hbm, sem, rows).wait()


def _rmsnorm_kernel(x_ref, g_ref, o_ref):
    x = x_ref[...]
    y = x * lax.rsqrt(jnp.mean(x * x, axis=-1, keepdims=True) + RMS_EPS)
    o_ref[...] = (y * g_ref[...]).astype(o_ref.dtype)


def _rmsnorm(x, g, out_dtype):
    m, d = x.shape
    tm = _tile(m, 512)
    return pl.pallas_call(
        _rmsnorm_kernel,
        grid=(m // tm,),
        in_specs=[pl.BlockSpec((tm, d), lambda i: (i, 0)), pl.BlockSpec((1, d), lambda i: (0, 0))],
        out_specs=pl.BlockSpec((tm, d), lambda i: (i, 0)),
        out_shape=jax.ShapeDtypeStruct((m, d), out_dtype),
        compiler_params=_params(("parallel",), 32),
        name="rmsnorm",
    )(x, g.reshape(1, d))


def _proj_kernel(a_ref, w_ref, *rest, sigmoid_bias):
    if sigmoid_bias:
        b_ref, o_ref = rest
    else:
        (o_ref,) = rest
    acc = jnp.dot(a_ref[...], w_ref[...], preferred_element_type=F32)
    if sigmoid_bias:
        acc = jax.nn.sigmoid(acc + b_ref[...])
    o_ref[...] = acc.astype(o_ref.dtype)


def _proj(a, w, col0, n, out_dtype, bias=None, tm=1024, tn=1024):
    m, k = a.shape
    tm, tn = _tile(m, tm), math.gcd(_tile(n, tn), col0)
    assert n % tn == 0 and tn % LANES == 0
    j0 = col0 // tn
    in_specs = [pl.BlockSpec((tm, k), lambda i, j: (i, 0)), pl.BlockSpec((k, tn), lambda i, j: (0, j + j0))]
    args = [a, w]
    if bias is not None:
        in_specs.append(pl.BlockSpec((1, tn), lambda i, j: (0, j)))
        args.append(bias.reshape(1, n))
    return pl.pallas_call(
        functools.partial(_proj_kernel, sigmoid_bias=bias is not None),
        grid=(m // tm, n // tn),
        in_specs=in_specs,
        out_specs=pl.BlockSpec((tm, tn), lambda i, j: (i, j)),
        out_shape=jax.ShapeDtypeStruct((m, n), out_dtype),
        compiler_params=_params(("parallel", "parallel"), 48),
        name="proj",
    )(*args)


def _t5_bucket(rel):
    nb = REL_BUCKETS // 2
    max_exact = nb // 2
    ret = jnp.where(rel > 0, nb, 0)
    n = jnp.abs(rel)
    nf = jnp.maximum(n, 1).astype(F32)
    large = max_exact + (jnp.log(nf / max_exact) / math.log(REL_MAX_DIST / max_exact)
                         * (nb - max_exact)).astype(jnp.int32)
    large = jnp.minimum(large, nb - 1)
    return ret + jnp.where(n < max_exact, n, large)


def _near_bias(rel_table, t):
    assert t + 1 >= REL_MAX_DIST
    table = rel_table.astype(F32)
    r = jnp.arange(t, dtype=jnp.int32)[:, None]
    c = jnp.arange(t, dtype=jnp.int32)[None, :]

    def lookup(bucket):
        out = jnp.zeros((A_HEADS,) + bucket.shape, F32)
        for bkt in range(REL_BUCKETS):
            out = jnp.where(bucket[None] == bkt, table[bkt][:, None, None], out)
        return out

    far = table[_t5_bucket(jnp.int32(-(t + 1)))][:, None, None]
    diag = lookup(_t5_bucket(c - r)) - far
    allowed = (c // CHUNK) <= (r // CHUNK)
    diag = jnp.where(allowed[None], diag * LOG2E, NEG_BIG)
    sub = (lookup(_t5_bucket(c - r - t)) - far) * LOG2E
    return jnp.stack([jnp.zeros_like(diag), sub, diag], axis=1)


def _attn_kernel(lam_ref, qa_ref, qb_ref, k_ref, v_ref, nb_ref, qg_ref, kg_ref, sg_ref, oa_ref, ob_ref,
                 kn_ref, vx_ref, m_ref, acc_ref, qs_ref, s_ref, *, t, nq, s_len, k_chunk, rg):
    g = pl.program_id(2)
    blk_a = g
    blk_b = nq - 1 - g
    hw = 2 * A_QK_DIM
    rows = 2 * t
    lo = lax.broadcasted_iota(jnp.int32, (1, hw), 1) < A_QK_DIM

    def qk_norm(x, gain):
        sq = x * x
        s_lo = jnp.sum(jnp.where(lo, sq, 0.0), axis=-1, keepdims=True)
        s_hi = jnp.sum(jnp.where(lo, 0.0, sq), axis=-1, keepdims=True)
        ms = jnp.where(lo, s_lo, s_hi) * (1.0 / A_QK_DIM)
        return x * lax.rsqrt(ms + RMS_EPS) * gain

    @pl.when(g == 0)
    def _():
        def body(c, carry):
            r0 = pl.multiple_of(c * k_chunk, k_chunk)
            kk = k_ref[0, pl.ds(r0, k_chunk), :].astype(F32)
            kn_ref[pl.ds(r0, k_chunk), :] = qk_norm(kk, kg_ref[...]).astype(BF16)
            vx_ref[pl.ds(r0, k_chunk), 0:hw] = v_ref[0, pl.ds(r0, k_chunk), :]
            vx_ref[pl.ds(r0, k_chunk), hw:2 * hw] = jnp.ones((k_chunk, hw), BF16)
            return carry
        lax.fori_loop(0, s_len // k_chunk, body, 0)

    for which, q_ref in ((0, qa_ref), (1, qb_ref)):
        q = qk_norm(q_ref[0].astype(F32), qg_ref[...]) * (A_QK_DIM ** -0.5 * LOG2E)
        qs_ref[which] = jnp.concatenate([jnp.where(lo, q, 0.0), jnp.where(lo, 0.0, q)], axis=0).astype(BF16)

    m_ref[...] = jnp.full(m_ref.shape, NEG_BIG, F32)
    acc_ref[...] = jnp.zeros(acc_ref.shape, F32)

    groups = [slice(c * rg, (c + 1) * rg) for c in range(rows // rg)]

    def item(i):
        which = (i > blk_a).astype(jnp.int32)
        qblk = jnp.where(i > blk_a, blk_b, blk_a)
        return which, qblk, i - which * (blk_a + 1)

    def logits(i, slot):
        which, qblk, kb = item(i)
        kj = kn_ref[pl.ds(pl.multiple_of(kb * t, t), t), :]
        tile = jnp.clip(kb - (qblk - 2), 0, 2)
        for rs in groups:
            s = lax.dot_general(qs_ref[which, rs, :], kj, (((1,), (1,)), ((), ())), preferred_element_type=F32)
            b0 = rs.start % t
            s_ref[slot, rs, :] = s + nb_ref[0, tile, b0:b0 + rg, :]

    def accumulate(i, slot):
        which, _, kb = item(i)
        vj = vx_ref[pl.ds(pl.multiple_of(kb * t, t), t), :]
        for rs in groups:
            s = s_ref[slot, rs, :]
            chunks = [s[:, c * LANES:(c + 1) * LANES] for c in range(t // LANES)]
            mc = chunks[0]
            for ch in chunks[1:]:
                mc = jnp.maximum(mc, ch)
            m_prev = m_ref[which, rs, :]
            m_new = jnp.maximum(m_prev, jnp.max(mc, axis=-1, keepdims=True))
            alpha = jnp.exp2(m_prev - m_new)
            p = jnp.concatenate([jnp.exp2(ch - m_new) for ch in chunks], axis=1).astype(BF16)
            pv = jnp.dot(p, vj, preferred_element_type=F32)
            acc_ref[which, rs, :] = jnp.concatenate([alpha, alpha], axis=1) * acc_ref[which, rs, :] + pv
            m_ref[which, rs, :] = m_new

    logits(0, 0)
    for i in range(nq + 1):
        accumulate(i, i % 2)
        if i < nq:
            logits(i + 1, (i + 1) % 2)

    lam = lam_ref[0]
    for which, o_ref in ((0, oa_ref), (1, ob_ref)):
        acc = acc_ref[which]
        o = acc[:t, :hw] / acc[:t, hw:] - lam * (acc[t:, :hw] / acc[t:, hw:])
        o = o * lax.rsqrt(jnp.mean(o * o, axis=-1, keepdims=True) + RMS_EPS) * sg_ref[...]
        o_ref[0] = (o * (1.0 - LAM_INIT)).astype(o_ref.dtype)


def _diff_attention(qkv, rel_table, lam, q_gain, k_gain, sub_gain):
    b, s_len, _ = qkv.shape
    t = _tile(s_len, ATT_Q_BLOCK)
    nq = s_len // t
    assert nq % 2 == 0
    half = nq // 2
    hw = 2 * A_QK_DIM
    nb = _near_bias(rel_table, t)
    tile2 = lambda gain: jnp.concatenate([gain, gain]).reshape(1, hw).astype(F32)
    kern = functools.partial(_attn_kernel, t=t, nq=nq, s_len=s_len, k_chunk=_tile(s_len, 512),
                             rg=_tile(t, ATT_ROW_GROUP))
    const = pl.BlockSpec((1, hw), lambda bi, h, g: (0, 0))
    o_lo, o_hi = pl.pallas_call(
        kern,
        grid=(b, A_HEADS, half),
        in_specs=[
            pl.BlockSpec(memory_space=pltpu.SMEM),
            pl.BlockSpec((1, t, hw), lambda bi, h, g: (bi, g, h)),
            pl.BlockSpec((1, t, hw), lambda bi, h, g: (bi, nq - 1 - g, h)),
            pl.BlockSpec((1, s_len, hw), lambda bi, h, g: (bi, 0, A_HEADS + h)),
            pl.BlockSpec((1, s_len, hw), lambda bi, h, g: (bi, 0, 2 * A_HEADS + h)),
            pl.BlockSpec((1, 3, t, t), lambda bi, h, g: (h, 0, 0, 0)),
            const, const, const,
        ],
        out_specs=[pl.BlockSpec((1, t, hw), lambda bi, h, g: (bi, g, h)),
                   pl.BlockSpec((1, t, hw), lambda bi, h, g: (bi, half - 1 - g, h))],
        out_shape=[jax.ShapeDtypeStruct((b, s_len // 2, A_HEADS * hw), BF16)] * 2,
        scratch_shapes=[
            pltpu.VMEM((s_len, hw), BF16),
            pltpu.VMEM((s_len, 2 * hw), BF16),
            pltpu.VMEM((2, 2 * t, hw), F32),
            pltpu.VMEM((2, 2 * t, 2 * hw), F32),
            pltpu.VMEM((2, 2 * t, hw), BF16),
            pltpu.VMEM((2, 2 * t, t), F32),
        ],
        compiler_params=_params(("arbitrary", "arbitrary", "arbitrary"), 48),
        name="diff_attention",
    )(lam.reshape(1).astype(F32), qkv, qkv, qkv, qkv, nb, tile2(q_gain), tile2(k_gain),
      sub_gain.reshape(1, hw).astype(F32))
    return jnp.concatenate([o_lo, o_hi], axis=1)


def _lru_kernel(xl_ref, gl_ref, cw_ref, cb_ref, wa_ref, ba_ref, wx_ref, bx_ref, c_ref, o_ref,
                xbuf, a_s, u_s, h_s, hc, *, ts, width):
    i = pl.program_id(1)
    halo = SUBLANES
    bd = width // LRU_BLOCKS

    @pl.when(i == 0)
    def _():
        xbuf[0:halo, :] = jnp.zeros((halo, width), F32)
        hc[...] = jnp.zeros(hc.shape, F32)

    x = xl_ref[0]
    xbuf[halo:halo + ts, :] = x
    xr = cb_ref[...] + cw_ref[0:1, :] * xbuf[pl.ds(halo - 3, ts), :]
    for j in range(1, CONV_WIDTH):
        xr = xr + cw_ref[j:j + 1, :] * xbuf[pl.ds(halo - 3 + j, ts), :]
    xbuf[0:halo, :] = x[ts - halo:, :]

    for g in range(LRU_BLOCKS):
        sl = slice(g * bd, (g + 1) * bd)
        xg = xr[:, sl]
        xb = xg.astype(BF16)
        r = jax.nn.sigmoid(jnp.dot(xb, wa_ref[g], preferred_element_type=F32) + ba_ref[:, sl])
        gi = jax.nn.sigmoid(jnp.dot(xb, wx_ref[g], preferred_element_type=F32) + bx_ref[:, sl])
        log_a = r * c_ref[:, sl]
        a = jnp.exp(log_a)
        a_s[:, sl] = a
        u_s[:, sl] = jnp.sqrt(-jnp.tanh(log_a) * (1.0 + a * a)) * (gi * xg)

    row = lax.broadcasted_iota(jnp.int32, (SUBLANES, width), 0)

    def body(gidx, h):
        r0 = pl.multiple_of(gidx * SUBLANES, SUBLANES)
        a = a_s[pl.ds(r0, SUBLANES), :]
        u = u_s[pl.ds(r0, SUBLANES), :]
        for d in (1, 2, 4):
            keep = row >= d
            u = jnp.where(keep, a * pltpu.roll(u, d, 0) + u, u)
            a = jnp.where(keep, a * pltpu.roll(a, d, 0), a)
        hr = a * h + u
        h_s[pl.ds(r0, SUBLANES), :] = hr
        return hr[SUBLANES - 1:SUBLANES, :]

    hc[...] = lax.fori_loop(0, ts // SUBLANES, body, hc[...], unroll=2)
    o_ref[0] = (h_s[...] * jax.nn.gelu(gl_ref[0])).astype(o_ref.dtype)


def _lru_branch(lru, conv_w, conv_b, wa, ba, wx, bx, lam_param):
    b, s_len, c2 = lru.shape
    width = c2 // 2
    ts = _tile(s_len, 256)
    bd = width // LRU_BLOCKS
    c_vec = (-LRU_C * jax.nn.softplus(-lam_param.astype(F32))).reshape(1, width)
    row = lambda v: v.reshape(1, width).astype(F32)
    const2 = lambda shape: pl.BlockSpec(shape, lambda bi, i: (0,) * len(shape))
    kern = functools.partial(_lru_kernel, ts=ts, width=width)
    return pl.pallas_call(
        kern,
        grid=(b, s_len // ts),
        in_specs=[
            pl.BlockSpec((1, ts, width), lambda bi, i: (bi, i, 0)),
            pl.BlockSpec((1, ts, width), lambda bi, i: (bi, i, 1)),
            const2((CONV_WIDTH, width)), const2((1, width)),
            const2((LRU_BLOCKS, bd, bd)), const2((1, width)),
            const2((LRU_BLOCKS, bd, bd)), const2((1, width)),
            const2((1, width)),
        ],
        out_specs=pl.BlockSpec((1, ts, width), lambda bi, i: (bi, i, 0)),
        out_shape=jax.ShapeDtypeStruct((b, s_len, width), BF16),
        scratch_shapes=[
            pltpu.VMEM((ts + SUBLANES, width), F32),
            pltpu.VMEM((ts, width), F32),
            pltpu.VMEM((ts, width), F32),
            pltpu.VMEM((ts, width), F32),
            pltpu.VMEM((1, width), F32),
        ],
        compiler_params=_params(("arbitrary", "arbitrary"), 40),
        name="rg_lru",
    )(lru, lru, conv_w.astype(F32), row(conv_b), wa.astype(BF16), row(ba), wx.astype(BF16), row(bx), c_vec)


def _merge_kernel(oa_ref, ob_ref, wa_ref, wb_ref, g0_ref, g1_ref, o_ref):
    ya = jnp.dot(oa_ref[...], wa_ref[...], preferred_element_type=F32)
    yb = jnp.dot(ob_ref[...], wb_ref[...], preferred_element_type=F32)
    o_ref[...] = (g0_ref[...].astype(F32) * ya + g1_ref[...].astype(F32) * yb).astype(o_ref.dtype)


def _merge(o_a, o_b, w_a, w_b, gates):
    m, ka = o_a.shape
    kb = o_b.shape[1]
    d = w_a.shape[1]
    tm, tn = _tile(m, 512), _tile(d, 1024)
    nj = d // tn
    return pl.pallas_call(
        _merge_kernel,
        grid=(m // tm, nj),
        in_specs=[
            pl.BlockSpec((tm, ka), lambda i, j: (i, 0)),
            pl.BlockSpec((tm, kb), lambda i, j: (i, 0)),
            pl.BlockSpec((ka, tn), lambda i, j: (0, j)),
            pl.BlockSpec((kb, tn), lambda i, j: (0, j)),
            pl.BlockSpec((tm, tn), lambda i, j: (i, j)),
            pl.BlockSpec((tm, tn), lambda i, j: (i, j + nj)),
        ],
        out_specs=pl.BlockSpec((tm, tn), lambda i, j: (i, j)),
        out_shape=jax.ShapeDtypeStruct((m, d), BF16),
        compiler_params=_params(("parallel", "parallel"), 40),
        name="branch_merge",
    )(o_a, o_b, w_a, w_b, gates, gates)


def _outproj_kernel(mx_ref, wo_ref, x_ref, g_ref, x1_ref, xn_ref, xn_hbm, nbuf, nsem, *, tm):
    i = pl.program_id(0)
    slot = i % 2
    _rows_out_reclaim(i, slot, nbuf, xn_hbm, nsem, tm)
    x1 = x_ref[...] + jnp.dot(mx_ref[...], wo_ref[...], preferred_element_type=F32)
    x1_ref[...] = x1
    xn = x1 * lax.rsqrt(jnp.mean(x1 * x1, axis=-1, keepdims=True) + RMS_EPS) * g_ref[...]
    xn_ref[...] = xn
    nbuf[slot] = xn
    _rows_out_send(i, pl.num_programs(0), slot, nbuf, xn_hbm, nsem, tm)


def _outproj(mixed, w_out, x, ln_ffn):
    m, d = x.shape
    tm = _tile(m, 512)
    row_blk = pl.BlockSpec((tm, d), lambda i: (i, 0))
    return pl.pallas_call(
        functools.partial(_outproj_kernel, tm=tm),
        grid=(m // tm,),
        in_specs=[row_blk, pl.BlockSpec((d, d), lambda i: (0, 0)), row_blk, pl.BlockSpec((1, d), lambda i: (0, 0))],
        out_specs=[row_blk, row_blk, pl.BlockSpec(memory_space=pl.ANY)],
        out_shape=[jax.ShapeDtypeStruct((m, d), F32), jax.ShapeDtypeStruct((m, d), F32),
                   jax.ShapeDtypeStruct((m, 1, d), F32)],
        scratch_shapes=[pltpu.VMEM((2, tm, d), F32), pltpu.SemaphoreType.DMA((2,))],
        compiler_params=_params(("arbitrary",), 60),
        name="outproj",
    )(mixed, w_out, x, ln_ffn.reshape(1, d).astype(F32))


def _router_kernel(xn_ref, w2_ref, wh_ref, br_ref, ti_ref, tg_ref, pos_ref, cnt_ref, carry, *, tm, sub):
    i = pl.program_id(0)

    @pl.when(i == 0)
    def _():
        carry[...] = jnp.zeros(carry.shape, F32)

    lane = lax.broadcasted_iota(jnp.int32, (sub, LANES), 1)
    rr = lax.broadcasted_iota(jnp.int32, (sub, sub), 0)
    cc = lax.broadcasted_iota(jnp.int32, (sub, sub), 1)
    below = (cc < rr).astype(BF16)

    for c in range(tm // sub):
        rs = slice(c * sub, (c + 1) * sub)
        xn = xn_ref[rs, :]
        xh = xn.astype(BF16)
        xl = (xn - xh.astype(F32)).astype(BF16)
        hi = jnp.dot(xh, w2_ref[...], preferred_element_type=F32)
        lo = jnp.dot(xl, wh_ref[...], preferred_element_type=F32)
        rest = hi[:, :LANES] + hi[:, LANES:] + lo + br_ref[...]

        vals, idxs = [], []
        for _ in range(TOP_K):
            mx = jnp.max(rest, axis=-1, keepdims=True)
            ix = jnp.min(jnp.where(rest == mx, lane, LANES), axis=-1, keepdims=True)
            vals.append(mx)
            idxs.append(ix)
            rest = jnp.where(lane == ix, NEG_SEL, rest)
        exps = [jnp.exp(v - vals[0]) for v in vals]
        den = exps[0]
        for e in exps[1:]:
            den = den + e
        ti = jnp.zeros((sub, LANES), jnp.int32)
        tg = jnp.zeros((sub, LANES), F32)
        sel = jnp.zeros((sub, LANES), F32)
        for k in range(TOP_K):
            ti = jnp.where(lane == k, idxs[k], ti)
            tg = jnp.where(lane == k, exps[k] / den, tg)
            sel = jnp.where(lane == idxs[k], 1.0, sel)
        ti_ref[rs, :] = ti
        tg_ref[rs, :] = tg

        pos = jnp.dot(below, sel.astype(BF16), preferred_element_type=F32) + carry[...]
        pos4 = jnp.zeros((sub, LANES), F32)
        for k in range(TOP_K):
            pk = jnp.sum(jnp.where(lane == idxs[k], pos, 0.0), axis=-1, keepdims=True)
            pos4 = jnp.where(lane == k, pk, pos4)
        pos_ref[rs, :] = pos4.astype(jnp.int32)
        carry[...] = carry[...] + jnp.sum(sel, axis=0, keepdims=True)
    cnt_ref[...] = carry[...].astype(jnp.int32)


def _router(xn, w_router, b_router):
    m, d = xn.shape
    e = w_router.shape[1]
    assert e <= LANES
    tm = _tile(m, 1024)
    wr = jnp.zeros((d, LANES), F32).at[:, :e].set(w_router.astype(F32))
    wh = wr.astype(BF16)
    wl = (wr - wh.astype(F32)).astype(BF16)
    br = jnp.full((1, LANES), NEG_BIG, F32).at[0, :e].set(b_router.astype(F32))
    row_blk = lambda w: pl.BlockSpec((tm, w), lambda i: (i, 0))
    const = lambda shape: pl.BlockSpec(shape, lambda i: (0, 0))
    return pl.pallas_call(
        functools.partial(_router_kernel, tm=tm, sub=_tile(tm, 256)),
        grid=(m // tm,),
        in_specs=[row_blk(d), const((d, 2 * LANES)), const((d, LANES)), const((1, LANES))],
        out_specs=[row_blk(LANES), row_blk(LANES), row_blk(LANES), const((1, LANES))],
        out_shape=[
            jax.ShapeDtypeStruct((m, LANES), jnp.int32),
            jax.ShapeDtypeStruct((m, LANES), F32),
            jax.ShapeDtypeStruct((m, LANES), jnp.int32),
            jax.ShapeDtypeStruct((1, LANES), jnp.int32),
        ],
        scratch_shapes=[pltpu.VMEM((1, LANES), F32)],
        compiler_params=_params(("arbitrary",), 40),
        name="router",
    )(xn, jnp.concatenate([wh, wl], axis=1), wh, br)


def _gather_kernel(nv_ref, idx_ref, nxt_ref, src_ref, *rest, rows, groups, weighted):
    if weighted:
        g_ref, base_ref, o_ref, buf, sem = rest
    else:
        o_ref, buf, sem = rest
    i = pl.program_id(0)
    n = nv_ref[0]
    total = rows * groups

    def row_copy(ref, r, slot):
        return pltpu.make_async_copy(src_ref.at[ref[0, 0, r]], buf.at[slot, pl.ds(r, 1), :], sem.at[slot])

    def issue(ref, static_slot):
        for r in range(total):
            row_copy(ref, r, static_slot).start()

    @pl.when(jnp.logical_and(i == 0, n > 0))
    def _():
        issue(idx_ref, 0)

    for s in range(2):
        @pl.when(jnp.logical_and(i + 1 < n, (i + 1) % 2 == s))
        def _():
            issue(nxt_ref, s)

    slot = i % 2

    @pl.when(i < n)
    def _():
        def wait_body(r, carry):
            row_copy(idx_ref, r, slot).wait()
            return carry
        lax.fori_loop(0, total, wait_body, 0, unroll=8)

        if weighted:
            lane = lax.broadcasted_iota(jnp.int32, (rows, LANES), 1)
            g = g_ref[...]
            acc = base_ref[...]
            for k in range(groups):
                gk = jnp.sum(jnp.where(lane == k, g, 0.0), axis=-1, keepdims=True)
                acc = acc + gk * buf[slot, pl.ds(k * rows, rows), :]
            o_ref[...] = acc.astype(o_ref.dtype)
        else:
            o_ref[...] = buf[slot].astype(o_ref.dtype)

    @pl.when(i >= n)
    def _():
        o_ref[...] = jnp.zeros(o_ref.shape, o_ref.dtype)


def _gather_rows(src, idx, n_valid, rows, out_dtype, gates=None, base=None):
    steps, _, total = idx.shape
    groups = total // rows
    d = src.shape[2]
    weighted = gates is not None
    in_specs = [
        pl.BlockSpec(memory_space=pltpu.SMEM),
        pl.BlockSpec((1, 1, total), lambda i: (i, 0, 0), memory_space=pltpu.SMEM),
        pl.BlockSpec((1, 1, total), lambda i: (jnp.minimum(i + 1, steps - 1), 0, 0), memory_space=pltpu.SMEM),
        pl.BlockSpec(memory_space=pl.ANY),
    ]
    args = [n_valid, idx, idx, src]
    if weighted:
        in_specs += [pl.BlockSpec((rows, LANES), lambda i: (i, 0)), pl.BlockSpec((rows, d), lambda i: (i, 0))]
        args += [gates, base]
    return pl.pallas_call(
        functools.partial(_gather_kernel, rows=rows, groups=groups, weighted=weighted),
        grid=(steps,),
        in_specs=in_specs,
        out_specs=pl.BlockSpec((rows, d), lambda i: (i, 0)),
        out_shape=jax.ShapeDtypeStruct((steps * rows, d), out_dtype),
        scratch_shapes=[pltpu.VMEM((2, total, d), F32), pltpu.SemaphoreType.DMA((2,))],
        compiler_params=_params(("arbitrary",), 48),
        name="combine_rows" if weighted else "gather_rows",
    )(*args)


def _zero_unused_block(nu_ref, o_ref):
    @pl.when(pl.program_id(1) >= nu_ref[0])
    def _():
        o_ref[...] = jnp.zeros(o_ref.shape, o_ref.dtype)


def _weight_tiles(sched_refs, w_hbm, wbuf, wsem, tn, convert):
    be_ref, _, first_ref, run_ref, nruns_ref, nexte_ref, nic_ref = sched_refs
    j = pl.program_id(0)
    i = pl.program_id(1)
    nj = pl.num_programs(0)

    def tile_copy(e, jj, slot):
        return pltpu.make_async_copy(w_hbm.at[e, :, pl.ds(pl.multiple_of(jj * tn, tn), tn)],
                                     wbuf.at[slot], wsem.at[slot])

    slot = (j * nruns_ref[0] + run_ref[i]) % 2

    @pl.when(jnp.logical_and(i == 0, j == 0))
    def _():
        tile_copy(be_ref[0], 0, 0).start()

    @pl.when(first_ref[i] == 1)
    def _():
        tile_copy(be_ref[i], j, slot).wait()
        convert(wbuf.at[slot])
        in_col = nic_ref[i] == 1

        @pl.when(jnp.logical_or(in_col, j + 1 < nj))
        def _():
            tile_copy(nexte_ref[i], jnp.where(in_col, j, j + 1), 1 - slot).start()


def _moe1_kernel(*refs, tn):
    sched_refs, (x_ref, w_hbm, bg_ref, bl_ref, o_ref, wp_ref, wbuf, wsem) = refs[:7], refs[7:]
    nu_ref = sched_refs[1]
    i = pl.program_id(1)
    groups = tn // MXU_DIM
    half = MXU_DIM // 2
    _zero_unused_block(nu_ref, o_ref)

    def convert(w_ref):
        rr = lax.broadcasted_iota(jnp.int32, (MXU_DIM, MXU_DIM), 0)
        cc = lax.broadcasted_iota(jnp.int32, (MXU_DIM, MXU_DIM), 1)
        src = jnp.where(cc < half, 2 * cc, 2 * (cc - half) + 1)
        perm = (rr == src).astype(BF16)
        for g in range(groups):
            sl = slice(g * MXU_DIM, (g + 1) * MXU_DIM)
            wt = w_ref[:, sl].astype(BF16)
            wp_ref[:, sl] = jnp.dot(wt, perm, preferred_element_type=F32).astype(BF16)

    _weight_tiles(sched_refs, w_hbm, wbuf, wsem, tn, convert)

    @pl.when(i < nu_ref[0])
    def _():
        h = jnp.dot(x_ref[...], wp_ref[...], preferred_element_type=F32)
        for g in range(groups):
            fs = slice(g * half, (g + 1) * half)
            hg = h[:, g * MXU_DIM:g * MXU_DIM + half] + bg_ref[0, :, fs]
            hl = h[:, g * MXU_DIM + half:(g + 1) * MXU_DIM] + bl_ref[0, :, fs]
            glu = jnp.minimum(hg, SWIGLU_LIMIT)
            lin = jnp.clip(hl, -SWIGLU_LIMIT, SWIGLU_LIMIT)
            o_ref[:, fs] = (glu * jax.nn.sigmoid(SWIGLU_ALPHA * glu) * (lin + 1.0)).astype(o_ref.dtype)


def _moe2_kernel(*refs, bm, tn):
    sched_refs, (a_ref, w_hbm, b_ref, o_hbm, wb_ref, wbuf, wsem, obuf, osem) = refs[:7], refs[7:]
    nu_ref = sched_refs[1]
    i = pl.program_id(1)
    slot = i % 2
    _rows_out_reclaim(i, slot, obuf, o_hbm, osem, bm)

    def convert(w_ref):
        wb_ref[...] = w_ref[...].astype(BF16)

    _weight_tiles(sched_refs, w_hbm, wbuf, wsem, tn, convert)

    @pl.when(i < nu_ref[0])
    def _():
        obuf[slot] = jnp.dot(a_ref[...], wb_ref[...], preferred_element_type=F32) + b_ref[0]

    @pl.when(i >= nu_ref[0])
    def _():
        obuf[slot] = jnp.zeros(obuf.shape[1:], obuf.dtype)

    _rows_out_send(i, pl.num_programs(1), slot, obuf, o_hbm, osem, bm)


def _grouped(kernel, name, rows_in, weight, biases, sched, bm, tn, out_cols_per_tile, out_dtype, token_rows_out=False):
    p, k = rows_in.shape
    n = weight.shape[2]
    nblk = p // bm
    tb = out_cols_per_tile
    blk = lambda i, nu: jnp.maximum(jnp.minimum(i, nu[0] - 1), 0)
    scratch = [pltpu.VMEM((k, tn), BF16), pltpu.VMEM((2, k, tn), F32), pltpu.SemaphoreType.DMA((2,))]
    if token_rows_out:
        assert n == tn == tb
        out_spec = pl.BlockSpec(memory_space=pl.ANY)
        out_shape = jax.ShapeDtypeStruct((p, 1, n), out_dtype)
        scratch += [pltpu.VMEM((2, bm, n), out_dtype), pltpu.SemaphoreType.DMA((2,))]
    else:
        out_spec = pl.BlockSpec((bm, tb), lambda j, i, *_: (i, j))
        out_shape = jax.ShapeDtypeStruct((p, (n // tn) * tb), out_dtype)
    grid_spec = pltpu.PrefetchScalarGridSpec(
        num_scalar_prefetch=len(sched),
        grid=(n // tn, nblk),
        in_specs=[pl.BlockSpec((bm, k), lambda j, i, be, nu, *_: (blk(i, nu), 0)),
                  pl.BlockSpec(memory_space=pl.ANY)]
        + [pl.BlockSpec((1, 1, tb), lambda j, i, be, *_: (be[i], 0, j))] * len(biases),
        out_specs=out_spec,
        scratch_shapes=scratch,
    )
    return pl.pallas_call(
        kernel,
        grid_spec=grid_spec,
        out_shape=out_shape,
        compiler_params=_params(("arbitrary", "arbitrary"), 60),
        name=name,
    )(*sched, rows_in, weight, *biases)


def _moe(x1, xn, ti, tg, pos, cnt, w1, b1, w2, b2):
    t_tok, d = x1.shape
    n_exp, _, f2 = w1.shape
    f = f2 // 2
    bm = MOE_ROWS
    assert (t_tok * TOP_K) % bm == 0
    nblk = t_tok * TOP_K // bm + n_exp
    p = nblk * bm

    ti4 = ti[:, :TOP_K]
    counts = cnt[0, :n_exp]
    padded = (counts + bm - 1) // bm * bm
    pad_ends = jnp.cumsum(padded)
    pad_starts = pad_ends - padded
    onehot = ti4[:, :, None] == jnp.arange(n_exp, dtype=jnp.int32)[None, None, :]
    dest4 = jnp.sum(jnp.where(onehot, pad_starts[None, None, :], 0), axis=-1) + pos[:, :TOP_K]
    n_used = (pad_ends[-1] // bm).astype(jnp.int32).reshape(1)
    blk_start = jnp.arange(nblk, dtype=jnp.int32) * bm
    blk_start = jnp.minimum(blk_start, pad_ends[-1] - bm)
    blk_expert = jnp.sum(blk_start[:, None] >= pad_ends[None, :], axis=1).astype(jnp.int32)
    blk_expert = jnp.minimum(blk_expert, n_exp - 1)
    first = jnp.concatenate([jnp.ones((1,), jnp.int32), (blk_expert[1:] != blk_expert[:-1]).astype(jnp.int32)])
    blk_ids = jnp.arange(nblk, dtype=jnp.int32)
    run_idx = (jnp.cumsum(first) - 1).astype(jnp.int32)
    n_runs = (run_idx[jnp.maximum(n_used[0] - 1, 0)] + 1).reshape(1)
    first_pos = jnp.where((first == 1) & (blk_ids < n_used[0]), blk_ids, nblk)
    later_first = jnp.concatenate([lax.cummin(first_pos[::-1])[::-1][1:], jnp.full((1,), nblk, jnp.int32)])
    next_in_col = (later_first < nblk).astype(jnp.int32)
    next_expert = jnp.where(later_first < nblk, blk_expert[jnp.minimum(later_first, nblk - 1)], blk_expert[0])
    sched = (blk_expert, n_used, first, run_idx, n_runs, next_expert.astype(jnp.int32), next_in_col)
    tok = jnp.broadcast_to(jnp.arange(t_tok, dtype=jnp.int32)[:, None], (t_tok, TOP_K))
    row_tok = (jnp.arange(p, dtype=jnp.int32) % t_tok).at[dest4.reshape(-1)].set(
        tok.reshape(-1), unique_indices=True, mode="promise_in_bounds")

    gr = _tile(bm, GATHER_ROWS)
    assert bm % gr == 0
    xg = _gather_rows(xn, row_tok.reshape(p // gr, 1, gr), n_used * (bm // gr), gr, BF16)

    b1g = b1[:, 0::2].reshape(n_exp, 1, f).astype(F32)
    b1l = b1[:, 1::2].reshape(n_exp, 1, f).astype(F32)
    tn1 = _tile(f2, EXPERT_UP_COLS)
    assert tn1 % MXU_DIM == 0
    act = _grouped(functools.partial(_moe1_kernel, tn=tn1), "expert_up", xg, w1, [b1g, b1l], sched,
                   bm, tn1, tn1 // 2, BF16)
    out = _grouped(functools.partial(_moe2_kernel, bm=bm, tn=d), "expert_down", act, w2,
                   [b2.reshape(n_exp, 1, d).astype(F32)], sched, bm, d, d, F32, token_rows_out=True)

    tc = _tile(t_tok, COMBINE_TOKENS)
    steps = t_tok // tc
    idx = jnp.transpose(dest4.reshape(steps, tc, TOP_K), (0, 2, 1)).reshape(steps, 1, TOP_K * tc)
    return _gather_rows(out, idx, jnp.full((1,), steps, jnp.int32), tc, F32, gates=tg, base=x1)


def kernel(x, ln_mix, w_in, b_gate, q_gain, k_gain, lambda_q1, lambda_k1, lambda_q2, lambda_k2, sub_gain, rel_table, conv_w, conv_b, lru_wa, lru_ba, lru_wx, lru_bx, lru_lambda, w_branch_a, w_branch_b, w_out, ln_ffn, w_router, b_router, w1, b1, w2, b2):
    b, s_len, d = x.shape
    t_tok = b * s_len
    depth = ln_mix.shape[0]
    assert depth == 1
    q_cols = A_HEADS * 2 * A_QK_DIM
    qkv_cols = 2 * q_cols + A_HEADS * A_V_DIM
    lru_w = d // 2
    xt = x.reshape(t_tok, d)
    l = 0

    hn = _rmsnorm(xt, ln_mix[l], BF16)
    w_in_b = w_in[l].astype(BF16)
    qkv = _proj(hn, w_in_b, 0, qkv_cols, BF16)
    lru = _proj(hn, w_in_b, qkv_cols, 2 * lru_w, F32)
    gates = _proj(hn, w_in_b, qkv_cols + 2 * lru_w, 2 * d, BF16, bias=b_gate[l])

    lam = (jnp.exp(jnp.sum(lambda_q1[l].astype(F32) * lambda_k1[l].astype(F32)))
           - jnp.exp(jnp.sum(lambda_q2[l].astype(F32) * lambda_k2[l].astype(F32))) + LAM_INIT)
    o_a = _diff_attention(qkv.reshape(b, s_len, qkv_cols), rel_table, lam, q_gain[l], k_gain[l], sub_gain[l])
    o_b = _lru_branch(lru.reshape(b, s_len, 2 * lru_w), conv_w[l], conv_b[l], lru_wa[l], lru_ba[l],
                      lru_wx[l], lru_bx[l], lru_lambda[l])

    mixed = _merge(o_a.reshape(t_tok, -1), o_b.reshape(t_tok, lru_w),
                   w_branch_a[l].astype(BF16), w_branch_b[l].astype(BF16), gates)
    x1, xn, xn_rows = _outproj(mixed, w_out[l].astype(BF16), xt, ln_ffn[l])
    ti, tg, pos, cnt = _router(xn, w_router[l], b_router[l])
    y = _moe(x1, xn_rows, ti, tg, pos, cnt, w1[l], b1[l], w2[l], b2[l])
    return y.reshape(b, s_len, d)
```

```python
import functools
import math

import jax
import jax.numpy as jnp
from jax import lax
from jax.experimental import pallas as pl
from jax.experimental.pallas import tpu as pltpu

F32 = jnp.float32
BF16 = jnp.bfloat16

CHUNK = 64
RMS_EPS = 1e-6
A_HEADS = 8
A_QK_DIM = 64
A_V_DIM = 2 * A_QK_DIM
LRU_BLOCKS = 8
CONV_WIDTH = 4
LRU_C = 8.0
REL_BUCKETS = 32
REL_MAX_DIST = 128
TOP_K = 4
SWIGLU_LIMIT = 7.0
SWIGLU_ALPHA = 1.702
LAM_INIT = 0.8 - 0.6 * math.exp(-0.3 * 0)
LOG2E = 1.4426950408889634

LANES = 128
SUBLANES = 8
MXU_DIM = 256
NEG_BIG = -1e30
NEG_SEL = -3e38

ATT_Q_BLOCK = 512
ATT_ROW_GROUP = 128
MOE_ROWS = 512
EXPERT_UP_COLS = 2048
GATHER_ROWS = 512
COMBINE_TOKENS = 256


def _tile(n, pref):
    t = min(n, pref)
    assert n % t == 0, (n, pref)
    return t


def _params(semantics, vmem_mib):
    return pltpu.CompilerParams(dimension_semantics=semantics, vmem_limit_bytes=vmem_mib * 1024 * 1024)


def _rows_out_copy(step, slot, buf, o_hbm, sem, rows):
    return pltpu.make_async_copy(buf.at[slot], o_hbm.at[pl.ds(step * rows, rows), 0, :], sem.at[slot])


def _rows_out_reclaim(i, slot, buf, o_hbm, sem, rows):
    @pl.when(i >= 2)
    def _():
        _rows_out_copy(i - 2, slot, buf, o_hbm, sem, rows).wait()


def _rows_out_send(i, n, slot, buf, o_hbm, sem, rows):
    _rows_out_copy(i, slot, buf, o_hbm, sem, rows).start()

    @pl.when(i == n - 1)
    def _():
        @pl.when(n > 1)
        def _():
            _rows_out_copy(i - 1, 1 - slot, buf, o_hbm, sem, rows).wait()
        _rows_out_copy(i, slot, buf, o_hbm, sem, rows).wait()


def _rmsnorm_kernel(x_ref, g_ref, o_ref):
    x = x_ref[...]
    y = x * lax.rsqrt(jnp.mean(x * x, axis=-1, keepdims=True) + RMS_EPS)
    o_ref[...] = (y * g_ref[...]).astype(o_ref.dtype)


def _rmsnorm(x, g, out_dtype):
    m, d = x.shape
    tm = _tile(m, 512)
    return pl.pallas_call(
        _rmsnorm_kernel,
        grid=(m // tm,),
        in_specs=[pl.BlockSpec((tm, d), lambda i: (i, 0)), pl.BlockSpec((1, d), lambda i: (0, 0))],
        out_specs=pl.BlockSpec((tm, d), lambda i: (i, 0)),
        out_shape=jax.ShapeDtypeStruct((m, d), out_dtype),
        compiler_params=_params(("parallel",), 32),
        name="rmsnorm",
    )(x, g.reshape(1, d))


def _proj_kernel(a_ref, w_ref, *rest, sigmoid_bias):
    if sigmoid_bias:
        b_ref, o_ref = rest
    else:
        (o_ref,) = rest
    acc = jnp.dot(a_ref[...], w_ref[...], preferred_element_type=F32)
    if sigmoid_bias:
        acc = jax.nn.sigmoid(acc + b_ref[...])
    o_ref[...] = acc.astype(o_ref.dtype)


def _proj(a, w, col0, n, out_dtype, bias=None, tm=1024, tn=1024):
    m, k = a.shape
    tm, tn = _tile(m, tm), math.gcd(_tile(n, tn), col0)
    assert n % tn == 0 and tn % LANES == 0
    j0 = col0 // tn
    in_specs = [pl.BlockSpec((tm, k), lambda i, j: (i, 0)), pl.BlockSpec((k, tn), lambda i, j: (0, j + j0))]
    args = [a, w]
    if bias is not None:
        in_specs.append(pl.BlockSpec((1, tn), lambda i, j: (0, j)))
        args.append(bias.reshape(1, n))
    return pl.pallas_call(
        functools.partial(_proj_kernel, sigmoid_bias=bias is not None),
        grid=(m // tm, n // tn),
        in_specs=in_specs,
        out_specs=pl.BlockSpec((tm, tn), lambda i, j: (i, j)),
        out_shape=jax.ShapeDtypeStruct((m, n), out_dtype),
        compiler_params=_params(("parallel", "parallel"), 48),
        name="proj",
    )(*args)


def _t5_bucket(rel):
    nb = REL_BUCKETS // 2
    max_exact = nb // 2
    ret = jnp.where(rel > 0, nb, 0)
    n = jnp.abs(rel)
    nf = jnp.maximum(n, 1).astype(F32)
    large = max_exact + (jnp.log(nf / max_exact) / math.log(REL_MAX_DIST / max_exact)
                         * (nb - max_exact)).astype(jnp.int32)
    large = jnp.minimum(large, nb - 1)
    return ret + jnp.where(n < max_exact, n, large)


def _near_bias(rel_table, t):
    assert t + 1 >= REL_MAX_DIST
    table = rel_table.astype(F32)
    r = jnp.arange(t, dtype=jnp.int32)[:, None]
    c = jnp.arange(t, dtype=jnp.int32)[None, :]

    def lookup(bucket):
        out = jnp.zeros((A_HEADS,) + bucket.shape, F32)
        for bkt in range(REL_BUCKETS):
            out = jnp.where(bucket[None] == bkt, table[bkt][:, None, None], out)
        return out

    far = table[_t5_bucket(jnp.int32(-(t + 1)))][:, None, None]
    diag = lookup(_t5_bucket(c - r)) - far
    allowed = (c // CHUNK) <= (r // CHUNK)
    diag = jnp.where(allowed[None], diag * LOG2E, NEG_BIG)
    sub = (lookup(_t5_bucket(c - r - t)) - far) * LOG2E
    return jnp.stack([jnp.zeros_like(diag), sub, diag], axis=1)


def _attn_kernel(lam_ref, qa_ref, qb_ref, k_ref, v_ref, nb_ref, qg_ref, kg_ref, sg_ref, oa_ref, ob_ref,
                 kn_ref, vx_ref, m_ref, acc_ref, qs_ref, s_ref, *, t, nq, s_len, k_chunk, rg):
    g = pl.program_id(2)
    blk_a = g
    blk_b = nq - 1 - g
    hw = 2 * A_QK_DIM
    rows = 2 * t
    lo = lax.broadcasted_iota(jnp.int32, (1, hw), 1) < A_QK_DIM

    def qk_norm(x, gain):
        sq = x * x
        s_lo = jnp.sum(jnp.where(lo, sq, 0.0), axis=-1, keepdims=True)
        s_hi = jnp.sum(jnp.where(lo, 0.0, sq), axis=-1, keepdims=True)
        ms = jnp.where(lo, s_lo, s_hi) * (1.0 / A_QK_DIM)
        return x * lax.rsqrt(ms + RMS_EPS) * gain

    @pl.when(g == 0)
    def _():
        def body(c, carry):
            r0 = pl.multiple_of(c * k_chunk, k_chunk)
            kk = k_ref[0, pl.ds(r0, k_chunk), :].astype(F32)
            kn_ref[pl.ds(r0, k_chunk), :] = qk_norm(kk, kg_ref[...]).astype(BF16)
            vx_ref[pl.ds(r0, k_chunk), 0:hw] = v_ref[0, pl.ds(r0, k_chunk), :]
            vx_ref[pl.ds(r0, k_chunk), hw:2 * hw] = jnp.ones((k_chunk, hw), BF16)
            return carry
        lax.fori_loop(0, s_len // k_chunk, body, 0)

    for which, q_ref in ((0, qa_ref), (1, qb_ref)):
        q = qk_norm(q_ref[0].astype(F32), qg_ref[...]) * (A_QK_DIM ** -0.5 * LOG2E)
        qs_ref[which] = jnp.concatenate([jnp.where(lo, q, 0.0), jnp.where(lo, 0.0, q)], axis=0).astype(BF16)

    m_ref[...] = jnp.full(m_ref.shape, NEG_BIG, F32)
    acc_ref[...] = jnp.zeros(acc_ref.shape, F32)

    groups = [slice(c * rg, (c + 1) * rg) for c in range(rows // rg)]

    def item(i):
        which = (i > blk_a).astype(jnp.int32)
        qblk = jnp.where(i > blk_a, blk_b, blk_a)
        return which, qblk, i - which * (blk_a + 1)

    def logits(i, slot):
        which, qblk, kb = item(i)
        kj = kn_ref[pl.ds(pl.multiple_of(kb * t, t), t), :]
        tile = jnp.clip(kb - (qblk - 2), 0, 2)
        for rs in groups:
            s = lax.dot_general(qs_ref[which, rs, :], kj, (((1,), (1,)), ((), ())), preferred_element_type=F32)
            b0 = rs.start % t
            s_ref[slot, rs, :] = s + nb_ref[0, tile, b0:b0 + rg, :]

    def accumulate(i, slot):
        which, _, kb = item(i)
        vj = vx_ref[pl.ds(pl.multiple_of(kb * t, t), t), :]
        for rs in groups:
            s = s_ref[slot, rs, :]
            chunks = [s[:, c * LANES:(c + 1) * LANES] for c in range(t // LANES)]
            mc = chunks[0]
            for ch in chunks[1:]:
                mc = jnp.maximum(mc, ch)
            m_prev = m_ref[which, rs, :]
            m_new = jnp.maximum(m_prev, jnp.max(mc, axis=-1, keepdims=True))
            alpha = jnp.exp2(m_prev - m_new)
            p = jnp.concatenate([jnp.exp2(ch - m_new) for ch in chunks], axis=1).astype(BF16)
            pv = jnp.dot(p, vj, preferred_element_type=F32)
            acc_ref[which, rs, :] = jnp.concatenate([alpha, alpha], axis=1) * acc_ref[which, rs, :] + pv
            m_ref[which, rs, :] = m_new

    logits(0, 0)
    for i in range(nq + 1):
        accumulate(i, i % 2)
        if i < nq:
            logits(i + 1, (i + 1) % 2)

    lam = lam_ref[0]
    for which, o_ref in ((0, oa_ref), (1, ob_ref)):
        acc = acc_ref[which]
        o = acc[:t, :hw] / acc[:t, hw:] - lam * (acc[t:, :hw] / acc[t:, hw:])
        o = o * lax.rsqrt(jnp.mean(o * o, axis=-1, keepdims=True) + RMS_EPS) * sg_ref[...]
        o_ref[0] = (o * (1.0 - LAM_INIT)).astype(o_ref.dtype)


def _diff_attention(qkv, rel_table, lam, q_gain, k_gain, sub_gain):
    b, s_len, _ = qkv.shape
    t = _tile(s_len, ATT_Q_BLOCK)
    nq = s_len // t
    assert nq % 2 == 0
    half = nq // 2
    hw = 2 * A_QK_DIM
    nb = _near_bias(rel_table, t)
    tile2 = lambda gain: jnp.concatenate([gain, gain]).reshape(1, hw).astype(F32)
    kern = functools.partial(_attn_kernel, t=t, nq=nq, s_len=s_len, k_chunk=_tile(s_len, 512),
                             rg=_tile(t, ATT_ROW_GROUP))
    const = pl.BlockSpec((1, hw), lambda bi, h, g: (0, 0))
    o_lo, o_hi = pl.pallas_call(
        kern,
        grid=(b, A_HEADS, half),
        in_specs=[
            pl.BlockSpec(memory_space=pltpu.SMEM),
            pl.BlockSpec((1, t, hw), lambda bi, h, g: (bi, g, h)),
            pl.BlockSpec((1, t, hw), lambda bi, h, g: (bi, nq - 1 - g, h)),
            pl.BlockSpec((1, s_len, hw), lambda bi, h, g: (bi, 0, A_HEADS + h)),
            pl.BlockSpec((1, s_len, hw), lambda bi, h, g: (bi, 0, 2 * A_HEADS + h)),
            pl.BlockSpec((1, 3, t, t), lambda bi, h, g: (h, 0, 0, 0)),
            const, const, const,
        ],
        out_specs=[pl.BlockSpec((1, t, hw), lambda bi, h, g: (bi, g, h)),
                   pl.BlockSpec((1, t, hw), lambda bi, h, g: (bi, half - 1 - g, h))],
        out_shape=[jax.ShapeDtypeStruct((b, s_len // 2, A_HEADS * hw), BF16)] * 2,
        scratch_shapes=[
            pltpu.VMEM((s_len, hw), BF16),
            pltpu.VMEM((s_len, 2 * hw), BF16),
            pltpu.VMEM((2, 2 * t, hw), F32),
            pltpu.VMEM((2, 2 * t, 2 * hw), F32),
            pltpu.VMEM((2, 2 * t, hw), BF16),
            pltpu.VMEM((2, 2 * t, t), F32),
        ],
        compiler_params=_params(("arbitrary", "arbitrary", "arbitrary"), 48),
        name="diff_attention",
    )(lam.reshape(1).astype(F32), qkv, qkv, qkv, qkv, nb, tile2(q_gain), tile2(k_gain),
      sub_gain.reshape(1, hw).astype(F32))
    return jnp.concatenate([o_lo, o_hi], axis=1)


def _lru_kernel(xl_ref, gl_ref, cw_ref, cb_ref, wa_ref, ba_ref, wx_ref, bx_ref, c_ref, o_ref,
                xbuf, a_s, u_s, h_s, hc, *, ts, width):
    i = pl.program_id(1)
    halo = SUBLANES
    bd = width // LRU_BLOCKS

    @pl.when(i == 0)
    def _():
        xbuf[0:halo, :] = jnp.zeros((halo, width), F32)
        hc[...] = jnp.zeros(hc.shape, F32)

    x = xl_ref[0]
    xbuf[halo:halo + ts, :] = x
    xr = cb_ref[...] + cw_ref[0:1, :] * xbuf[pl.ds(halo - 3, ts), :]
    for j in range(1, CONV_WIDTH):
        xr = xr + cw_ref[j:j + 1, :] * xbuf[pl.ds(halo - 3 + j, ts), :]
    xbuf[0:halo, :] = x[ts - halo:, :]

    for g in range(LRU_BLOCKS):
        sl = slice(g * bd, (g + 1) * bd)
        xg = xr[:, sl]
        xb = xg.astype(BF16)
        r = jax.nn.sigmoid(jnp.dot(xb, wa_ref[g], preferred_element_type=F32) + ba_ref[:, sl])
        gi = jax.nn.sigmoid(jnp.dot(xb, wx_ref[g], preferred_element_type=F32) + bx_ref[:, sl])
        log_a = r * c_ref[:, sl]
        a = jnp.exp(log_a)
        a_s[:, sl] = a
        u_s[:, sl] = jnp.sqrt(-jnp.tanh(log_a) * (1.0 + a * a)) * (gi * xg)

    row = lax.broadcasted_iota(jnp.int32, (SUBLANES, width), 0)

    def body(gidx, h):
        r0 = pl.multiple_of(gidx * SUBLANES, SUBLANES)
        a = a_s[pl.ds(r0, SUBLANES), :]
        u = u_s[pl.ds(r0, SUBLANES), :]
        for d in (1, 2, 4):
            keep = row >= d
            u = jnp.where(keep, a * pltpu.roll(u, d, 0) + u, u)
            a = jnp.where(keep, a * pltpu.roll(a, d, 0), a)
        hr = a * h + u
        h_s[pl.ds(r0, SUBLANES), :] = hr
        return hr[SUBLANES - 1:SUBLANES, :]

    hc[...] = lax.fori_loop(0, ts // SUBLANES, body, hc[...], unroll=2)
    o_ref[0] = (h_s[...] * jax.nn.gelu(gl_ref[0])).astype(o_ref.dtype)


def _lru_branch(lru, conv_w, conv_b, wa, ba, wx, bx, lam_param):
    b, s_len, c2 = lru.shape
    width = c2 // 2
    ts = _tile(s_len, 256)
    bd = width // LRU_BLOCKS
    c_vec = (-LRU_C * jax.nn.softplus(-lam_param.astype(F32))).reshape(1, width)
    row = lambda v: v.reshape(1, width).astype(F32)
    const2 = lambda shape: pl.BlockSpec(shape, lambda bi, i: (0,) * len(shape))
    kern = functools.partial(_lru_kernel, ts=ts, width=width)
    return pl.pallas_call(
        kern,
        grid=(b, s_len // ts),
        in_specs=[
            pl.BlockSpec((1, ts, width), lambda bi, i: (bi, i, 0)),
            pl.BlockSpec((1, ts, width), lambda bi, i: (bi, i, 1)),
            const2((CONV_WIDTH, width)), const2((1, width)),
            const2((LRU_BLOCKS, bd, bd)), const2((1, width)),
            const2((LRU_BLOCKS, bd, bd)), const2((1, width)),
            const2((1, width)),
        ],
        out_specs=pl.BlockSpec((1, ts, width), lambda bi, i: (bi, i, 0)),
        out_shape=jax.ShapeDtypeStruct((b, s_len, width), BF16),
        scratch_shapes=[
            pltpu.VMEM((ts + SUBLANES, width), F32),
            pltpu.VMEM((ts, width), F32),
            pltpu.VMEM((ts, width), F32),
            pltpu.VMEM((ts, width), F32),
            pltpu.VMEM((1, width), F32),
        ],
        compiler_params=_params(("arbitrary", "arbitrary"), 40),
        name="rg_lru",
    )(lru, lru, conv_w.astype(F32), row(conv_b), wa.astype(BF16), row(ba), wx.astype(BF16), row(bx), c_vec)


def _merge_kernel(oa_ref, ob_ref, wa_ref, wb_ref, g0_ref, g1_ref, o_ref):
    ya = jnp.dot(oa_ref[...], wa_ref[...], preferred_element_type=F32)
    yb = jnp.dot(ob_ref[...], wb_ref[...], preferred_element_type=F32)
    o_ref[...] = (g0_ref[...].astype(F32) * ya + g1_ref[...].astype(F32) * yb).astype(o_ref.dtype)


def _merge(o_a, o_b, w_a, w_b, gates):
    m, ka = o_a.shape
    kb = o_b.shape[1]
    d = w_a.shape[1]
    tm, tn = _tile(m, 512), _tile(d, 1024)
    nj = d // tn
    return pl.pallas_call(
        _merge_kernel,
        grid=(m // tm, nj),
        in_specs=[
            pl.BlockSpec((tm, ka), lambda i, j: (i, 0)),
            pl.BlockSpec((tm, kb), lambda i, j: (i, 0)),
            pl.BlockSpec((ka, tn), lambda i, j: (0, j)),
            pl.BlockSpec((kb, tn), lambda i, j: (0, j)),
            pl.BlockSpec((tm, tn), lambda i, j: (i, j)),
            pl.BlockSpec((tm, tn), lambda i, j: (i, j + nj)),
        ],
        out_specs=pl.BlockSpec((tm, tn), lambda i, j: (i, j)),
        out_shape=jax.ShapeDtypeStruct((m, d), BF16),
        compiler_params=_params(("parallel", "parallel"), 40),
        name="branch_merge",
    )(o_a, o_b, w_a, w_b, gates, gates)


def _outproj_kernel(mx_ref, wo_ref, x_ref, g_ref, x1_ref, xn_ref, xn_hbm, nbuf, nsem, *, tm):
    i = pl.program_id(0)
    slot = i % 2
    _rows_out_reclaim(i, slot, nbuf, xn_hbm, nsem, tm)
    x1 = x_ref[...] + jnp.dot(mx_ref[...], wo_ref[...], preferred_element_type=F32)
    x1_ref[...] = x1
    xn = x1 * lax.rsqrt(jnp.mean(x1 * x1, axis=-1, keepdims=True) + RMS_EPS) * g_ref[...]
    xn_ref[...] = xn
    nbuf[slot] = xn
    _rows_out_send(i, pl.num_programs(0), slot, nbuf, xn_hbm, nsem, tm)


def _outproj(mixed, w_out, x, ln_ffn):
    m, d = x.shape
    tm = _tile(m, 512)
    row_blk = pl.BlockSpec((tm, d), lambda i: (i, 0))
    return pl.pallas_call(
        functools.partial(_outproj_kernel, tm=tm),
        grid=(m // tm,),
        in_specs=[row_blk, pl.BlockSpec((d, d), lambda i: (0, 0)), row_blk, pl.BlockSpec((1, d), lambda i: (0, 0))],
        out_specs=[row_blk, row_blk, pl.BlockSpec(memory_space=pl.ANY)],
        out_shape=[jax.ShapeDtypeStruct((m, d), F32), jax.ShapeDtypeStruct((m, d), F32),
                   jax.ShapeDtypeStruct((m, 1, d), F32)],
        scratch_shapes=[pltpu.VMEM((2, tm, d), F32), pltpu.SemaphoreType.DMA((2,))],
        compiler_params=_params(("arbitrary",), 60),
        name="outproj",
    )(mixed, w_out, x, ln_ffn.reshape(1, d).astype(F32))


def _router_kernel(xn_ref, w2_ref, wh_ref, br_ref, ti_ref, tg_ref, pos_ref, cnt_ref, carry, *, tm, sub):
    i = pl.program_id(0)

    @pl.when(i == 0)
    def _():
        carry[...] = jnp.zeros(carry.shape, F32)

    lane = lax.broadcasted_iota(jnp.int32, (sub, LANES), 1)
    rr = lax.broadcasted_iota(jnp.int32, (sub, sub), 0)
    cc = lax.broadcasted_iota(jnp.int32, (sub, sub), 1)
    below = (cc < rr).astype(BF16)

    for c in range(tm // sub):
        rs = slice(c * sub, (c + 1) * sub)
        xn = xn_ref[rs, :]
        xh = xn.astype(BF16)
        xl = (xn - xh.astype(F32)).astype(BF16)
        hi = jnp.dot(xh, w2_ref[...], preferred_element_type=F32)
        lo = jnp.dot(xl, wh_ref[...], preferred_element_type=F32)
        rest = hi[:, :LANES] + hi[:, LANES:] + lo + br_ref[...]

        vals, idxs = [], []
        for _ in range(TOP_K):
            mx = jnp.max(rest, axis=-1, keepdims=True)
            ix = jnp.min(jnp.where(rest == mx, lane, LANES), axis=-1, keepdims=True)
            vals.append(mx)
            idxs.append(ix)
            rest = jnp.where(lane == ix, NEG_SEL, rest)
        exps = [jnp.exp(v - vals[0]) for v in vals]
        den = exps[0]
        for e in exps[1:]:
            den = den + e
        ti = jnp.zeros((sub, LANES), jnp.int32)
        tg = jnp.zeros((sub, LANES), F32)
        sel = jnp.zeros((sub, LANES), F32)
        for k in range(TOP_K):
            ti = jnp.where(lane == k, idxs[k], ti)
            tg = jnp.where(lane == k, exps[k] / den, tg)
            sel = jnp.where(lane == idxs[k], 1.0, sel)
        ti_ref[rs, :] = ti
        tg_ref[rs, :] = tg

        pos = jnp.dot(below, sel.astype(BF16), preferred_element_type=F32) + carry[...]
        pos4 = jnp.zeros((sub, LANES), F32)
        for k in range(TOP_K):
            pk = jnp.sum(jnp.where(lane == idxs[k], pos, 0.0), axis=-1, keepdims=True)
            pos4 = jnp.where(lane == k, pk, pos4)
        pos_ref[rs, :] = pos4.astype(jnp.int32)
        carry[...] = carry[...] + jnp.sum(sel, axis=0, keepdims=True)
    cnt_ref[...] = carry[...].astype(jnp.int32)


def _router(xn, w_router, b_router):
    m, d = xn.shape
    e = w_router.shape[1]
    assert e <= LANES
    tm = _tile(m, 1024)
    wr = jnp.zeros((d, LANES), F32).at[:, :e].set(w_router.astype(F32))
    wh = wr.astype(BF16)
    wl = (wr - wh.astype(F32)).astype(BF16)
    br = jnp.full((1, LANES), NEG_BIG, F32).at[0, :e].set(b_router.astype(F32))
    row_blk = lambda w: pl.BlockSpec((tm, w), lambda i: (i, 0))
    const = lambda shape: pl.BlockSpec(shape, lambda i: (0, 0))
    return pl.pallas_call(
        functools.partial(_router_kernel, tm=tm, sub=_tile(tm, 256)),
        grid=(m // tm,),
        in_specs=[row_blk(d), const((d, 2 * LANES)), const((d, LANES)), const((1, LANES))],
        out_specs=[row_blk(LANES), row_blk(LANES), row_blk(LANES), const((1, LANES))],
        out_shape=[
            jax.ShapeDtypeStruct((m, LANES), jnp.int32),
            jax.ShapeDtypeStruct((m, LANES), F32),
            jax.ShapeDtypeStruct((m, LANES), jnp.int32),
            jax.ShapeDtypeStruct((1, LANES), jnp.int32),
        ],
        scratch_shapes=[pltpu.VMEM((1, LANES), F32)],
        compiler_params=_params(("arbitrary",), 40),
        name="router",
    )(xn, jnp.concatenate([wh, wl], axis=1), wh, br)


def _gather_kernel(nv_ref, idx_ref, nxt_ref, src_ref, *rest, rows, groups, weighted):
    if weighted:
        g_ref, base_ref, o_ref, buf, sem = rest
    else:
        o_ref, buf, sem = rest
    i = pl.program_id(0)
    n = nv_ref[0]
    total = rows * groups

    def row_copy(ref, r, slot):
        return pltpu.make_async_copy(src_ref.at[ref[0, 0, r]], buf.at[slot, pl.ds(r, 1), :], sem.at[slot])

    def issue(ref, static_slot):
        for r in range(total):
            row_copy(ref, r, static_slot).start()

    @pl.when(jnp.logical_and(i == 0, n > 0))
    def _():
        issue(idx_ref, 0)

    for s in range(2):
        @pl.when(jnp.logical_and(i + 1 < n, (i + 1) % 2 == s))
        def _():
            issue(nxt_ref, s)

    slot = i % 2

    @pl.when(i < n)
    def _():
        def wait_body(r, carry):
            row_copy(idx_ref, r, slot).wait()
            return carry
        lax.fori_loop(0, total, wait_body, 0, unroll=8)

        if weighted:
            lane = lax.broadcasted_iota(jnp.int32, (rows, LANES), 1)
            g = g_ref[...]
            acc = base_ref[...]
            for k in range(groups):
                gk = jnp.sum(jnp.where(lane == k, g, 0.0), axis=-1, keepdims=True)
                acc = acc + gk * buf[slot, pl.ds(k * rows, rows), :]
            o_ref[...] = acc.astype(o_ref.dtype)
        else:
            o_ref[...] = buf[slot].astype(o_ref.dtype)

    @pl.when(i >= n)
    def _():
        o_ref[...] = jnp.zeros(o_ref.shape, o_ref.dtype)


def _gather_rows(src, idx, n_valid, rows, out_dtype, gates=None, base=None):
    steps, _, total = idx.shape
    groups = total // rows
    d = src.shape[2]
    weighted = gates is not None
    in_specs = [
        pl.BlockSpec(memory_space=pltpu.SMEM),
        pl.BlockSpec((1, 1, total), lambda i: (i, 0, 0), memory_space=pltpu.SMEM),
        pl.BlockSpec((1, 1, total), lambda i: (jnp.minimum(i + 1, steps - 1), 0, 0), memory_space=pltpu.SMEM),
        pl.BlockSpec(memory_space=pl.ANY),
    ]
    args = [n_valid, idx, idx, src]
    if weighted:
        in_specs += [pl.BlockSpec((rows, LANES), lambda i: (i, 0)), pl.BlockSpec((rows, d), lambda i: (i, 0))]
        args += [gates, base]
    return pl.pallas_call(
        functools.partial(_gather_kernel, rows=rows, groups=groups, weighted=weighted),
        grid=(steps,),
        in_specs=in_specs,
        out_specs=pl.BlockSpec((rows, d), lambda i: (i, 0)),
        out_shape=jax.ShapeDtypeStruct((steps * rows, d), out_dtype),
        scratch_shapes=[pltpu.VMEM((2, total, d), F32), pltpu.SemaphoreType.DMA((2,))],
        compiler_params=_params(("arbitrary",), 48),
        name="combine_rows" if weighted else "gather_rows",
    )(*args)


def _zero_unused_block(nu_ref, o_ref):
    @pl.when(pl.program_id(1) >= nu_ref[0])
    def _():
        o_ref[...] = jnp.zeros(o_ref.shape, o_ref.dtype)


def _weight_tiles(sched_refs, w_hbm, wbuf, wsem, tn, convert):
    be_ref, _, first_ref, run_ref, nruns_ref, nexte_ref, nic_ref = sched_refs
    j = pl.program_id(0)
    i = pl.program_id(1)
    nj = pl.num_programs(0)

    def tile_copy(e, jj, slot):
        return pltpu.make_async_copy(w_hbm.at[e, :, pl.ds(pl.multiple_of(jj * tn, tn), tn)],
                                     wbuf.at[slot], wsem.at[slot])

    slot = (j * nruns_ref[0] + run_ref[i]) % 2

    @pl.when(jnp.logical_and(i == 0, j == 0))
    def _():
        tile_copy(be_ref[0], 0, 0).start()

    @pl.when(first_ref[i] == 1)
    def _():
        tile_copy(be_ref[i], j, slot).wait()
        convert(wbuf.at[slot])
        in_col = nic_ref[i] == 1

        @pl.when(jnp.logical_or(in_col, j + 1 < nj))
        def _():
            tile_copy(nexte_ref[i], jnp.where(in_col, j, j + 1), 1 - slot).start()


def _moe1_kernel(*refs, tn):
    sched_refs, (x_ref, w_hbm, bg_ref, bl_ref, o_ref, wp_ref, wbuf, wsem) = refs[:7], refs[7:]
    nu_ref = sched_refs[1]
    i = pl.program_id(1)
    groups = tn // MXU_DIM
    half = MXU_DIM // 2
    _zero_unused_block(nu_ref, o_ref)

    def convert(w_ref):
        rr = lax.broadcasted_iota(jnp.int32, (MXU_DIM, MXU_DIM), 0)
        cc = lax.broadcasted_iota(jnp.int32, (MXU_DIM, MXU_DIM), 1)
        src = jnp.where(cc < half, 2 * cc, 2 * (cc - half) + 1)
        perm = (rr == src).astype(BF16)
        for g in range(groups):
            sl = slice(g * MXU_DIM, (g + 1) * MXU_DIM)
            wt = w_ref[:, sl].astype(BF16)
            wp_ref[:, sl] = jnp.dot(wt, perm, preferred_element_type=F32).astype(BF16)

    _weight_tiles(sched_refs, w_hbm, wbuf, wsem, tn, convert)

    @pl.when(i < nu_ref[0])
    def _():
        h = jnp.dot(x_ref[...], wp_ref[...], preferred_element_type=F32)
        for g in range(groups):
            fs = slice(g * half, (g + 1) * half)
            hg = h[:, g * MXU_DIM:g * MXU_DIM + half] + bg_ref[0, :, fs]
            hl = h[:, g * MXU_DIM + half:(g + 1) * MXU_DIM] + bl_ref[0, :, fs]
            glu = jnp.minimum(hg, SWIGLU_LIMIT)
            lin = jnp.clip(hl, -SWIGLU_LIMIT, SWIGLU_LIMIT)
            o_ref[:, fs] = (glu * jax.nn.sigmoid(SWIGLU_ALPHA * glu) * (lin + 1.0)).astype(o_ref.dtype)


def _moe2_kernel(*refs, bm, tn):
    sched_refs, (a_ref, w_hbm, b_ref, o_hbm, wb_ref, wbuf, wsem, obuf, osem) = refs[:7], refs[7:]
    nu_ref = sched_refs[1]
    i = pl.program_id(1)
    slot = i % 2
    _rows_out_reclaim(i, slot, obuf, o_hbm, osem, bm)

    def convert(w_ref):
        wb_ref[...] = w_ref[...].astype(BF16)

    _weight_tiles(sched_refs, w_hbm, wbuf, wsem, tn, convert)

    @pl.when(i < nu_ref[0])
    def _():
        obuf[slot] = jnp.dot(a_ref[...], wb_ref[...], preferred_element_type=F32) + b_ref[0]

    @pl.when(i >= nu_ref[0])
    def _():
        obuf[slot] = jnp.zeros(obuf.shape[1:], obuf.dtype)

    _rows_out_send(i, pl.num_programs(1), slot, obuf, o_hbm, osem, bm)


def _grouped(kernel, name, rows_in, weight, biases, sched, bm, tn, out_cols_per_tile, out_dtype, token_rows_out=False):
    p, k = rows_in.shape
    n = weight.shape[2]
    nblk = p // bm
    tb = out_cols_per_tile
    blk = lambda i, nu: jnp.maximum(jnp.minimum(i, nu[0] - 1), 0)
    scratch = [pltpu.VMEM((k, tn), BF16), pltpu.VMEM((2, k, tn), F32), pltpu.SemaphoreType.DMA((2,))]
    if token_rows_out:
        assert n == tn == tb
        out_spec = pl.BlockSpec(memory_space=pl.ANY)
        out_shape = jax.ShapeDtypeStruct((p, 1, n), out_dtype)
        scratch += [pltpu.VMEM((2, bm, n), out_dtype), pltpu.SemaphoreType.DMA((2,))]
    else:
        out_spec = pl.BlockSpec((bm, tb), lambda j, i, *_: (i, j))
        out_shape = jax.ShapeDtypeStruct((p, (n // tn) * tb), out_dtype)
    grid_spec = pltpu.PrefetchScalarGridSpec(
        num_scalar_prefetch=len(sched),
        grid=(n // tn, nblk),
        in_specs=[pl.BlockSpec((bm, k), lambda j, i, be, nu, *_: (blk(i, nu), 0)),
                  pl.BlockSpec(memory_space=pl.ANY)]
        + [pl.BlockSpec((1, 1, tb), lambda j, i, be, *_: (be[i], 0, j))] * len(biases),
        out_specs=out_spec,
        scratch_shapes=scratch,
    )
    return pl.pallas_call(
        kernel,
        grid_spec=grid_spec,
        out_shape=out_shape,
        compiler_params=_params(("arbitrary", "arbitrary"), 60),
        name=name,
    )(*sched, rows_in, weight, *biases)


def _moe(x1, xn, ti, tg, pos, cnt, w1, b1, w2, b2):
    t_tok, d = x1.shape
    n_exp, _, f2 = w1.shape
    f = f2 // 2
    bm = MOE_ROWS
    assert (t_tok * TOP_K) % bm == 0
    nblk = t_tok * TOP_K // bm + n_exp
    p = nblk * bm

    ti4 = ti[:, :TOP_K]
    counts = cnt[0, :n_exp]
    padded = (counts + bm - 1) // bm * bm
    pad_ends = jnp.cumsum(padded)
    pad_starts = pad_ends - padded
    onehot = ti4[:, :, None] == jnp.arange(n_exp, dtype=jnp.int32)[None, None, :]
    dest4 = jnp.sum(jnp.where(onehot, pad_starts[None, None, :], 0), axis=-1) + pos[:, :TOP_K]
    n_used = (pad_ends[-1] // bm).astype(jnp.int32).reshape(1)
    blk_start = jnp.arange(nblk, dtype=jnp.int32) * bm
    blk_start = jnp.minimum(blk_start, pad_ends[-1] - bm)
    blk_expert = jnp.sum(blk_start[:, None] >= pad_ends[None, :], axis=1).astype(jnp.int32)
    blk_expert = jnp.minimum(blk_expert, n_exp - 1)
    first = jnp.concatenate([jnp.ones((1,), jnp.int32), (blk_expert[1:] != blk_expert[:-1]).astype(jnp.int32)])
    blk_ids = jnp.arange(nblk, dtype=jnp.int32)
    run_idx = (jnp.cumsum(first) - 1).astype(jnp.int32)
    n_runs = (run_idx[jnp.maximum(n_used[0] - 1, 0)] + 1).reshape(1)
    first_pos = jnp.where((first == 1) & (blk_ids < n_used[0]), blk_ids, nblk)
    later_first = jnp.concatenate([lax.cummin(first_pos[::-1])[::-1][1:], jnp.full((1,), nblk, jnp.int32)])
    next_in_col = (later_first < nblk).astype(jnp.int32)
    next_expert = jnp.where(later_first < nblk, blk_expert[jnp.minimum(later_first, nblk - 1)], blk_expert[0])
    sched = (blk_expert, n_used, first, run_idx, n_runs, next_expert.astype(jnp.int32), next_in_col)
    tok = jnp.broadcast_to(jnp.arange(t_tok, dtype=jnp.int32)[:, None], (t_tok, TOP_K))
    row_tok = (jnp.arange(p, dtype=jnp.int32) % t_tok).at[dest4.reshape(-1)].set(
        tok.reshape(-1), unique_indices=True, mode="promise_in_bounds")

    gr = _tile(bm, GATHER_ROWS)
    assert bm % gr == 0
    xg = _gather_rows(xn, row_tok.reshape(p // gr, 1, gr), n_used * (bm // gr), gr, BF16)

    b1g = b1[:, 0::2].reshape(n_exp, 1, f).astype(F32)
    b1l = b1[:, 1::2].reshape(n_exp, 1, f).astype(F32)
    tn1 = _tile(f2, EXPERT_UP_COLS)
    assert tn1 % MXU_DIM == 0
    act = _grouped(functools.partial(_moe1_kernel, tn=tn1), "expert_up", xg, w1, [b1g, b1l], sched,
                   bm, tn1, tn1 // 2, BF16)
    out = _grouped(functools.partial(_moe2_kernel, bm=bm, tn=d), "expert_down", act, w2,
                   [b2.reshape(n_exp, 1, d).astype(F32)], sched, bm, d, d, F32, token_rows_out=True)

    tc = _tile(t_tok, COMBINE_TOKENS)
    steps = t_tok // tc
    idx = jnp.transpose(dest4.reshape(steps, tc, TOP_K), (0, 2, 1)).reshape(steps, 1, TOP_K * tc)
    return _gather_rows(out, idx, jnp.full((1,), steps, jnp.int32), tc, F32, gates=tg, base=x1)


def kernel(x, ln_mix, w_in, b_gate, q_gain, k_gain, lambda_q1, lambda_k1, lambda_q2, lambda_k2, sub_gain, rel_table, conv_w, conv_b, lru_wa, lru_ba, lru_wx, lru_bx, lru_lambda, w_branch_a, w_branch_b, w_out, ln_ffn, w_router, b_router, w1, b1, w2, b2):
    b, s_len, d = x.shape
    t_tok = b * s_len
    depth = ln_mix.shape[0]
    assert depth == 1
    q_cols = A_HEADS * 2 * A_QK_DIM
    qkv_cols = 2 * q_cols + A_HEADS * A_V_DIM
    lru_w = d // 2
    xt = x.reshape(t_tok, d)
    l = 0

    hn = _rmsnorm(xt, ln_mix[l], BF16)
    w_in_b = w_in[l].astype(BF16)
    qkv = _proj(hn, w_in_b, 0, qkv_cols, BF16)
    lru = _proj(hn, w_in_b, qkv_cols, 2 * lru_w, F32)
    gates = _proj(hn, w_in_b, qkv_cols + 2 * lru_w, 2 * d, BF16, bias=b_gate[l])

    lam = (jnp.exp(jnp.sum(lambda_q1[l].astype(F32) * lambda_k1[l].astype(F32)))
           - jnp.exp(jnp.sum(lambda_q2[l].astype(F32) * lambda_k2[l].astype(F32))) + LAM_INIT)
    o_a = _diff_attention(qkv.reshape(b, s_len, qkv_cols), rel_table, lam, q_gain[l], k_gain[l], sub_gain[l])
    o_b = _lru_branch(lru.reshape(b, s_len, 2 * lru_w), conv_w[l], conv_b[l], lru_wa[l], lru_ba[l],
                      lru_wx[l], lru_bx[l], lru_lambda[l])

    mixed = _merge(o_a.reshape(t_tok, -1), o_b.reshape(t_tok, lru_w),
                   w_branch_a[l].astype(BF16), w_branch_b[l].astype(BF16), gates)
    x1, xn, xn_rows = _outproj(mixed, w_out[l].astype(BF16), xt, ln_ffn[l])
    ti, tg, pos, cnt = _router(xn, w_router[l], b_router[l])
    y = _moe(x1, xn_rows, ti, tg, pos, cnt, w1[l], b1[l], w2[l], b2[l])
    return y.reshape(b, s_len, d)
```
